```python
import jax
import jax.numpy as jnp
from jax import lax
import numpy as np

D_MODEL = 1024
BATCH = 1
SEQ = 16384
DEPTH = 1
DEC_BATCH = 128
DEC_SEQ = 4
PAST_LEN = 16384
PAGE_SIZE = 128

HG_HEADS = 4
HG_DK = 128
HG_DV = 128
HG_CHUNK = 64
HG_QW = HG_HEADS * HG_DK
HG_VW = HG_HEADS * HG_DV
MLA_HEADS = 4
MLA_NOPE = 128
MLA_ROPE = 64
MLA_V = 128
MLA_Q_RANK = 384
MLA_KV_RANK = 256
MLA_SCALE = (MLA_NOPE + MLA_ROPE) ** -0.5
ROPE_THETA = 10000.0
Q_BLOCK = 128
N_EXPERTS = 32
TOP_K = 4
D_FF = 1024
SWIGLU_ALPHA = 1.702
SWIGLU_LIMIT = 7.0
MOE_BLOCK = 128
PLE_DIM = 256
EPS = 1e-6

SPLIT_SIZES = (HG_QW, HG_QW, HG_VW, HG_VW, MLA_Q_RANK, MLA_KV_RANK, MLA_ROPE, D_MODEL, D_MODEL)
IN_COLS = 2 * HG_QW + 2 * HG_VW + MLA_Q_RANK + MLA_KV_RANK + MLA_ROPE + 2 * D_MODEL

kernel_name = 'hgrn2_mla_moe_hybrid_step'


def rms_norm(x, g):
    xf = x.astype(jnp.float32)
    y = xf * lax.rsqrt(jnp.mean(xf * xf, axis=-1, keepdims=True) + EPS)
    return (y * g.astype(jnp.float32)).astype(x.dtype)


def rope(x, pos):
    half = x.shape[-1] // 2
    inv = ROPE_THETA ** (-jnp.arange(half, dtype=jnp.float32) / half)
    ang = pos.astype(jnp.float32)[:, None] * inv[None, :]
    cos = jnp.cos(ang)[:, None, :]
    sin = jnp.sin(ang)[:, None, :]
    xf = x.astype(jnp.float32)
    x1, x2 = xf[..., :half], xf[..., half:]
    return jnp.concatenate([x1 * cos - x2 * sin, x1 * sin + x2 * cos], axis=-1).astype(x.dtype)


def split_cols(z):
    out, start = [], 0
    for size in SPLIT_SIZES:
        out.append(z[..., start:start + size])
        start += size
    return out


def to_chunks(a, c):
    n, t, h, d = a.shape
    return a.reshape(n, t // c, c, h, d).transpose(1, 0, 3, 2, 4)


def hgrn2_recurrence(q, k, v, logf, s0):
    n, t, h, _ = q.shape
    c = HG_CHUNK if t % HG_CHUNK == 0 else t
    causal = (jnp.arange(c)[:, None] >= jnp.arange(c)[None, :])[:, :, None]

    def step(s, blk):
        qc, kc, vc, lc = blk
        b = jnp.cumsum(lc, axis=2)
        inter = jnp.einsum('nhtd,nhde->nhte', qc * jnp.exp(b), s)
        rel = jnp.exp(jnp.where(causal, b[:, :, :, None, :] - b[:, :, None, :, :], -jnp.inf))
        att = jnp.einsum('nhtd,nhtsd,nhsd->nhts', qc, rel, kc)
        o = inter + jnp.einsum('nhts,nhse->nhte', att, vc)
        b_end = b[:, :, -1:, :]
        s_new = (jnp.exp(b_end[:, :, 0, :])[..., None] * s
                 + jnp.einsum('nhsd,nhse->nhde', kc * jnp.exp(b_end - b), vc))
        return s_new, o

    s_fin, o = lax.scan(step, s0, (to_chunks(q, c), to_chunks(k, c), to_chunks(v, c), to_chunks(logf, c)))
    o = o.transpose(1, 0, 3, 2, 4).reshape(n, t, h, -1)
    return o, s_fin


def hgrn2_branch(zq, zf, zi, zg, lb, g_onorm, s0):
    n, t, _ = zq.shape
    f32 = jnp.float32
    shp_k = (n, t, HG_HEADS, HG_DK)
    shp_v = (n, t, HG_HEADS, HG_DV)
    fgate = lb + (1.0 - lb) * jax.nn.sigmoid(zf.astype(f32))
    q = zq.astype(f32).reshape(shp_k) * HG_DK ** -0.5
    k = (1.0 - fgate).reshape(shp_k)
    logf = jnp.log(fgate).reshape(shp_k)
    v = zi.astype(f32).reshape(shp_v)
    o, s_fin = hgrn2_recurrence(q, k, v, logf, s0.astype(f32))
    o = rms_norm(o, g_onorm) * jax.nn.silu(zg.astype(f32).reshape(shp_v))
    return o.reshape(n, t, HG_VW).astype(zq.dtype), s_fin.astype(s0.dtype)


def mla_logits(q_lat, q_pe, c, kpe):
    s = jnp.einsum('nthr,nlr->nhtl', q_lat, c, preferred_element_type=jnp.float32)
    s = s + jnp.einsum('nthp,nlp->nhtl', q_pe, kpe, preferred_element_type=jnp.float32)
    return s * MLA_SCALE


def mla_attend_prompt(q_lat, q_pe, c, kpe):
    n, t, h, r = q_lat.shape
    qb = min(Q_BLOCK, t)
    k_pos = jnp.arange(t)

    def block(start):
        ql = lax.dynamic_slice_in_dim(q_lat, start, qb, axis=1)
        qp = lax.dynamic_slice_in_dim(q_pe, start, qb, axis=1)
        q_pos = start + jnp.arange(qb)
        s = jnp.where(k_pos[None, :] <= q_pos[:, None], mla_logits(ql, qp, c, kpe), -jnp.inf)
        w = jax.nn.softmax(s, axis=-1).astype(c.dtype)
        return jnp.einsum('nhtl,nlr->nthr', w, c)

    out = lax.map(block, jnp.arange(t // qb) * qb)
    return out.transpose(1, 0, 2, 3, 4).reshape(n, t, h, r)


def mla_attend_sample(q_lat, q_pe, c_new, kpe_new, c_past, kpe_past):
    t = q_lat.shape[1]
    past = c_past.shape[1]
    s_past = mla_logits(q_lat, q_pe, c_past, kpe_past)
    causal = jnp.arange(t)[:, None] >= jnp.arange(t)[None, :]
    s_new = jnp.where(causal, mla_logits(q_lat, q_pe, c_new, kpe_new), -jnp.inf)
    w = jax.nn.softmax(jnp.concatenate([s_past, s_new], axis=-1), axis=-1).astype(c_new.dtype)
    return (jnp.einsum('nhtl,nlr->nthr', w[..., :past], c_past)
            + jnp.einsum('nhts,nsr->nthr', w[..., past:], c_new))


def moe(h, w_router, b_router, w1, b1, w2, b2):
    m, d = h.shape
    logits = jnp.dot(h, w_router, preferred_element_type=jnp.float32) + b_router.astype(jnp.float32)
    top_v, top_i = lax.top_k(logits, TOP_K)
    gates = jax.nn.softmax(top_v, axis=-1)
    a = m * TOP_K
    flat_e = top_i.reshape(a)
    flat_tok = jnp.arange(a, dtype=jnp.int32) // TOP_K
    flat_g = gates.reshape(a)
    order = jnp.argsort(flat_e)
    se = flat_e[order]
    counts = jnp.bincount(flat_e, length=N_EXPERTS)
    padded = (counts + MOE_BLOCK - 1) // MOE_BLOCK * MOE_BLOCK
    pad_end = jnp.cumsum(padded)
    pad_start = pad_end - padded
    grp_start = jnp.cumsum(counts) - counts
    dest = pad_start[se] + jnp.arange(a) - grp_start[se]
    n_blocks = -(-a // MOE_BLOCK) + N_EXPERTS
    rows = n_blocks * MOE_BLOCK
    row_tok = jnp.zeros((rows,), jnp.int32).at[dest].set(flat_tok[order])
    row_g = jnp.zeros((rows,), jnp.float32).at[dest].set(flat_g[order])
    blk_e = jnp.minimum(jnp.searchsorted(pad_end, jnp.arange(n_blocks) * MOE_BLOCK, side='right'),
                        N_EXPERTS - 1)

    def run_block(args):
        tok, e = args
        u = h[tok] @ w1[e] + b1[e]
        glu = jnp.minimum(u[:, 0::2], SWIGLU_LIMIT)
        lin = jnp.clip(u[:, 1::2], -SWIGLU_LIMIT, SWIGLU_LIMIT)
        act = glu * jax.nn.sigmoid(SWIGLU_ALPHA * glu) * (lin + 1.0)
        return act @ w2[e] + b2[e]

    out = lax.map(run_block, (row_tok.reshape(n_blocks, MOE_BLOCK), blk_e)).reshape(rows, d)
    out = out * row_g[:, None].astype(h.dtype)
    return jnp.zeros_like(h).at[row_tok].add(out)


def layer(x, p, pos, s0, c_past, kpe_past, lb, g_mix, w_in, g_qnorm, w_uq, w_uk, w_uv, g_kvnorm,
          g_onorm, w_pa, w_pb, w_out, g_ffn, w_router, b_router, w1, b1, w2, b2, g_ple, w_ple, w_pg):
    n, t, d = x.shape
    h = rms_norm(x, g_mix)
    zq, zf, zi, zg, zcq, zckv, zkpe, zga, zgb = split_cols(h @ w_in)
    o_a, s_new = hgrn2_branch(zq, zf, zi, zg, lb, g_onorm, s0)
    q = (rms_norm(zcq, g_qnorm) @ w_uq).reshape(n, t, MLA_HEADS, MLA_NOPE + MLA_ROPE)
    q_lat = jnp.einsum('nthd,rhd->nthr', q[..., :MLA_NOPE], w_uk)
    q_pe = rope(q[..., MLA_NOPE:], pos)
    c_new = rms_norm(zckv, g_kvnorm)
    kpe_new = rope(zkpe[:, :, None, :], pos)[:, :, 0, :]
    if c_past is None:
        o_lat = mla_attend_prompt(q_lat, q_pe, c_new, kpe_new)
    else:
        o_lat = mla_attend_sample(q_lat, q_pe, c_new, kpe_new, c_past, kpe_past)
    o_b = jnp.einsum('nthr,rhv->nthv', o_lat, w_uv).reshape(n, t, MLA_HEADS * MLA_V)
    mix = jax.nn.sigmoid(zga) * (o_a @ w_pa) + jax.nn.sigmoid(zgb) * (o_b @ w_pb)
    x = x + mix @ w_out
    x = x + moe(rms_norm(x, g_ffn).reshape(n * t, d), w_router, b_router, w1, b1, w2, b2).reshape(n, t, d)
    x = x + (p @ w_ple) * jax.nn.sigmoid(rms_norm(x, g_ple) @ w_pg)
    return x, c_new, kpe_new, s_new


def setup_inputs(seed: int = 0) -> dict:
    key = jax.random.key(seed)
    ks = jax.random.split(key, 32)
    f32 = jnp.float32
    n_pages = PAST_LEN // PAGE_SIZE
    n_used = DEC_BATCH * n_pages
    n_pool = n_used + max(1, n_used // 4)

    def nrm(k, shape, scale):
        return jax.random.normal(k, shape, f32) * scale

    def gain(k, shape):
        return 1.0 + 0.02 * jax.random.normal(k, shape, f32)

    page_table = jax.random.permutation(ks[5], n_pool)[:n_used].reshape(DEC_BATCH, n_pages).astype(jnp.int32)
    return {
        'x_prompt': nrm(ks[0], (BATCH, SEQ, D_MODEL), 1.0),
        'x_sample': nrm(ks[1], (DEC_BATCH, DEC_SEQ, D_MODEL), 1.0),
        'cache_ckv': nrm(ks[2], (DEPTH, n_pool, PAGE_SIZE, MLA_KV_RANK), 1.0),
        'cache_kpe': nrm(ks[3], (DEPTH, n_pool, PAGE_SIZE, MLA_ROPE), 1.0),
        'state_hgrn': nrm(ks[4], (DEPTH, DEC_BATCH, HG_HEADS, HG_DK, HG_DV), 0.5),
        'page_table': page_table,
        'p_prompt': nrm(ks[6], (DEPTH, BATCH, SEQ, PLE_DIM), 1.0),
        'p_sample': nrm(ks[7], (DEPTH, DEC_BATCH, DEC_SEQ, PLE_DIM), 1.0),
        'hg_lb': 1.0 + 0.1 * jax.random.normal(ks[8], (DEPTH + 1, HG_QW), f32),
        'g_mix': gain(ks[9], (DEPTH, D_MODEL)),
        'w_in': nrm(ks[10], (DEPTH, D_MODEL, IN_COLS), D_MODEL ** -0.5),
        'g_qnorm': gain(ks[11], (DEPTH, MLA_Q_RANK)),
        'w_uq': nrm(ks[12], (DEPTH, MLA_Q_RANK, MLA_HEADS * (MLA_NOPE + MLA_ROPE)), MLA_Q_RANK ** -0.5),
        'w_uk': nrm(ks[13], (DEPTH, MLA_KV_RANK, MLA_HEADS, MLA_NOPE), MLA_KV_RANK ** -0.5),
        'w_uv': nrm(ks[14], (DEPTH, MLA_KV_RANK, MLA_HEADS, MLA_V), MLA_KV_RANK ** -0.5),
        'g_kvnorm': gain(ks[15], (DEPTH, MLA_KV_RANK)),
        'g_onorm': gain(ks[16], (DEPTH, HG_DV)),
        'w_pa': nrm(ks[17], (DEPTH, HG_VW, D_MODEL), HG_VW ** -0.5),
        'w_pb': nrm(ks[18], (DEPTH, MLA_HEADS * MLA_V, D_MODEL), (MLA_HEADS * MLA_V) ** -0.5),
        'w_out': nrm(ks[19], (DEPTH, D_MODEL, D_MODEL), D_MODEL ** -0.5),
        'g_ffn': gain(ks[20], (DEPTH, D_MODEL)),
        'w_router': nrm(ks[21], (DEPTH, D_MODEL, N_EXPERTS), D_MODEL ** -0.5),
        'b_router': nrm(ks[22], (DEPTH, N_EXPERTS), 0.01),
        'w1': nrm(ks[23], (DEPTH, N_EXPERTS, D_MODEL, 2 * D_FF), D_MODEL ** -0.5),
        'b1': nrm(ks[24], (DEPTH, N_EXPERTS, 2 * D_FF), 0.01),
        'w2': nrm(ks[25], (DEPTH, N_EXPERTS, D_FF, D_MODEL), D_FF ** -0.5),
        'b2': nrm(ks[26], (DEPTH, N_EXPERTS, D_MODEL), 0.01),
        'g_ple': gain(ks[27], (DEPTH, D_MODEL)),
        'w_ple': nrm(ks[28], (DEPTH, PLE_DIM, D_MODEL), PLE_DIM ** -0.5),
        'w_pg': nrm(ks[29], (DEPTH, D_MODEL, D_MODEL), D_MODEL ** -0.5),
        'g_final': gain(ks[30], (D_MODEL,)),
    }


def reference(x_prompt, x_sample, cache_ckv, cache_kpe, state_hgrn, page_table, p_prompt, p_sample,
              hg_lb, g_mix, w_in, g_qnorm, w_uq, w_uk, w_uv, g_kvnorm, g_onorm, w_pa, w_pb, w_out,
              g_ffn, w_router, b_router, w1, b1, w2, b2, g_ple, w_ple, w_pg, g_final):
    n_p, t_p, _ = x_prompt.shape
    n_s, t_s, _ = x_sample.shape
    past = page_table.shape[1] * cache_ckv.shape[2]
    pos_prompt = jnp.arange(t_p)
    pos_sample = past + jnp.arange(t_s)
    lb_all = jnp.cumsum(jax.nn.softmax(hg_lb.astype(jnp.float32), axis=0), axis=0)
    s0_prompt = jnp.zeros((n_p, HG_HEADS, HG_DK, HG_DV), state_hgrn.dtype)
    xp, xs = x_prompt, x_sample
    ckv_p, kpe_p, st_p, ckv_s, kpe_s, st_s = [], [], [], [], [], []
    for i in range(DEPTH):
        lw = dict(g_mix=g_mix[i], w_in=w_in[i], g_qnorm=g_qnorm[i], w_uq=w_uq[i], w_uk=w_uk[i],
                  w_uv=w_uv[i], g_kvnorm=g_kvnorm[i], g_onorm=g_onorm[i], w_pa=w_pa[i], w_pb=w_pb[i],
                  w_out=w_out[i], g_ffn=g_ffn[i], w_router=w_router[i], b_router=b_router[i],
                  w1=w1[i], b1=b1[i], w2=w2[i], b2=b2[i], g_ple=g_ple[i], w_ple=w_ple[i], w_pg=w_pg[i])
        xp, c1, k1, s1 = layer(xp, p_prompt[i], pos_prompt, s0_prompt, None, None, lb_all[i], **lw)
        c_past = cache_ckv[i, page_table].reshape(n_s, past, MLA_KV_RANK)
        k_past = cache_kpe[i, page_table].reshape(n_s, past, MLA_ROPE)
        xs, c2, k2, s2 = layer(xs, p_sample[i], pos_sample, state_hgrn[i], c_past, k_past, lb_all[i], **lw)
        ckv_p.append(c1)
        kpe_p.append(k1)
        st_p.append(s1)
        ckv_s.append(c2)
        kpe_s.append(k2)
        st_s.append(s2)
    y_prompt = rms_norm(xp, g_final)
    y_sample = rms_norm(xs, g_final)
    return (y_prompt, y_sample, jnp.stack(ckv_p), jnp.stack(kpe_p), jnp.stack(st_p),
            jnp.stack(ckv_s), jnp.stack(kpe_s), jnp.stack(st_s))
```

```python
import functools

import numpy as np
import jax
import jax.numpy as jnp
from jax import lax
from jax.experimental import pallas as pl
from jax.experimental.pallas import tpu as pltpu

F32 = jnp.float32
BF16 = jnp.bfloat16

HG_HEADS = 4
HG_DK = 128
HG_DV = 128
MLA_HEADS = 4
MLA_NOPE = 128
MLA_ROPE = 64
MLA_V = 128
MLA_Q_RANK = 384
MLA_KV_RANK = 256
MLA_SCALE = (MLA_NOPE + MLA_ROPE) ** -0.5
ROPE_THETA = 10000.0
N_EXPERTS = 32
TOP_K = 4
SWIGLU_ALPHA = 1.702
SWIGLU_LIMIT = 7.0
EPS = 1e-6

LANES = 128
SUBLANES = 8
VMEM_LIMIT = 56 * 1024 * 1024

Q_TILE = 128
MOE_ROWS = 256
NEG_INF = float("-inf")


def _cparams(sem):
    return pltpu.CompilerParams(dimension_semantics=sem, vmem_limit_bytes=VMEM_LIMIT)


def _dot(a, b):
    return jnp.dot(a, b, preferred_element_type=F32)


def _dot_nt(a, b):
    return lax.dot_general(a, b, (((1,), (1,)), ((), ())), preferred_element_type=F32)


def _dot_tn(a, b):
    return lax.dot_general(a, b, (((0,), (0,)), ((), ())), preferred_element_type=F32)


def _rms(x, g):
    return x * lax.rsqrt(jnp.mean(x * x, axis=-1, keepdims=True) + EPS) * g


def _sigmoid(x):
    return 1.0 / (1.0 + jnp.exp(-x))


def _split3(x):
    hi = x.astype(BF16)
    r1 = x - hi.astype(F32)
    mid = r1.astype(BF16)
    lo = (r1 - mid.astype(F32)).astype(BF16)
    return hi, mid, lo


def _rope64(v, cos2, sin2):
    half = MLA_ROPE // 2
    partner = jnp.concatenate([v[:, half:], v[:, :half]], axis=1)
    return v * cos2 + partner * sin2


def _inproj_kernel(x_ref, cos_ref, sin_ref, gmix_ref, wh_ref, wmla_ref, wg_ref, gq_ref, wuq_ref,
                   wukt_ref, gkv_ref,
                   zh_ref, gates_ref, c_ref, kpe_ref, qlat_ref, qpe_ref, kcb_ref, kpeb_ref):
    tm = x_ref.shape[0]
    h = _rms(x_ref[...], gmix_ref[...]).astype(BF16)
    zh_ref[...] = _dot(h, wh_ref[...])
    gates_ref[...] = _sigmoid(_dot(h, wg_ref[...]))
    zm = _dot(h, wmla_ref[...])
    c = _rms(zm[:, MLA_Q_RANK:MLA_Q_RANK + MLA_KV_RANK], gkv_ref[...])
    c_ref[...] = c
    kcb_ref[...] = c.astype(BF16)
    cos2 = cos_ref[...]
    sin2 = sin_ref[...]
    kpe = _rope64(zm[:, MLA_Q_RANK + MLA_KV_RANK:], cos2, sin2)
    kpe_ref[...] = kpe
    kpeb_ref[...] = kpe.astype(BF16)
    qn = _rms(zm[:, :MLA_Q_RANK], gq_ref[...]).astype(BF16)
    q = _dot(qn, wuq_ref[...])
    for hh in range(MLA_HEADS):
        q_nope = q[:, hh * MLA_NOPE:(hh + 1) * MLA_NOPE].astype(BF16)
        qlat = (_dot(q_nope, wukt_ref[hh]) * MLA_SCALE).astype(BF16)
        off = MLA_HEADS * MLA_NOPE + hh * MLA_ROPE
        qpe = (_rope64(q[:, off:off + MLA_ROPE], cos2, sin2) * MLA_SCALE).astype(BF16)
        for tb in range(tm // Q_TILE):
            qlat_ref[tb, hh] = qlat[tb * Q_TILE:(tb + 1) * Q_TILE]
            qpe_ref[tb, hh] = qpe[tb * Q_TILE:(tb + 1) * Q_TILE]


def _inproj(x_all, cos2, sin2, g_mix, w_h, w_mla, w_g, g_q, w_uq, w_ukt, g_kv, tm):
    n, d = x_all.shape
    nq = n // Q_TILE
    row = lambda i: (i, 0)
    const2 = lambda i: (0, 0)
    const3 = lambda i: (0, 0, 0)
    full = lambda a: pl.BlockSpec(a.shape, const2 if a.ndim == 2 else const3)
    out_shape = (
        jax.ShapeDtypeStruct((n, w_h.shape[1]), F32),
        jax.ShapeDtypeStruct((n, w_g.shape[1]), F32),
        jax.ShapeDtypeStruct((n, MLA_KV_RANK), F32),
        jax.ShapeDtypeStruct((n, MLA_ROPE), F32),
        jax.ShapeDtypeStruct((nq, MLA_HEADS, Q_TILE, MLA_KV_RANK), BF16),
        jax.ShapeDtypeStruct((nq, MLA_HEADS, Q_TILE, MLA_ROPE), BF16),
        jax.ShapeDtypeStruct((n, MLA_KV_RANK), BF16),
        jax.ShapeDtypeStruct((n, MLA_ROPE), BF16),
    )
    tq = tm // Q_TILE
    out_specs = (
        pl.BlockSpec((tm, w_h.shape[1]), row),
        pl.BlockSpec((tm, w_g.shape[1]), row),
        pl.BlockSpec((tm, MLA_KV_RANK), row),
        pl.BlockSpec((tm, MLA_ROPE), row),
        pl.BlockSpec((tq, MLA_HEADS, Q_TILE, MLA_KV_RANK), lambda i: (i, 0, 0, 0)),
        pl.BlockSpec((tq, MLA_HEADS, Q_TILE, MLA_ROPE), lambda i: (i, 0, 0, 0)),
        pl.BlockSpec((tm, MLA_KV_RANK), row),
        pl.BlockSpec((tm, MLA_ROPE), row),
    )
    in_specs = [
        pl.BlockSpec((tm, d), row),
        pl.BlockSpec((tm, MLA_ROPE), row),
        pl.BlockSpec((tm, MLA_ROPE), row),
        full(g_mix), full(w_h), full(w_mla), full(w_g), full(g_q), full(w_uq), full(w_ukt), full(g_kv),
    ]
    return pl.pallas_call(
        _inproj_kernel,
        out_shape=out_shape,
        grid=(n // tm,),
        in_specs=in_specs,
        out_specs=out_specs,
        compiler_params=_cparams(("parallel",)),
        name="inproj",
    )(x_all, cos2, sin2, g_mix, w_h, w_mla, w_g, g_q, w_uq, w_ukt, g_kv)


HG_DIAG = 8


def _hgrn_levels(chunk):
    levels, m = [], chunk // 2
    while m >= HG_DIAG:
        levels.append(m)
        m //= 2
    return tuple(levels)


def _hgrn_mats(chunk):
    r = np.arange(chunk)[:, None]
    j = np.arange(chunk)[None, :]
    mats = [j <= r, j > r]
    for m in _hgrn_levels(chunk):
        same = (r // (2 * m)) == (j // (2 * m))
        second = (r % (2 * m)) >= m
        mid = (r // (2 * m)) * 2 * m + m
        a = second & same & (j >= mid) & (j <= r)
        b = (~second) & same & (j > r) & (j < mid)
        mats.append(a | b)
    return np.concatenate(mats, axis=0).astype(np.float32)


def _hgrn_kernel(zh_ref, mats_ref, lb_ref, gon_ref, s0_ref, o_ref, sout_ref, st_ref, *, chunk, t_valid):
    ci = pl.program_id(1)
    levels = _hgrn_levels(chunk)
    qw = HG_HEADS * HG_DK

    @pl.when(ci == 0)
    def _():
        for h in range(HG_HEADS):
            st_ref[h] = s0_ref[0, h].T

    zh = zh_ref[0]
    zq, zf, zi, zg = zh[:, :qw], zh[:, qw:2 * qw], zh[:, 2 * qw:3 * qw], zh[:, 3 * qw:]
    lb = lb_ref[...]
    f = lb + (1.0 - lb) * _sigmoid(zf)
    k = 1.0 - f
    logf = jnp.log(f)
    row = lax.broadcasted_iota(jnp.int32, (chunk, 1), 0)
    if t_valid < chunk:
        live = row < t_valid
        logf = jnp.where(live, logf, 0.0)
        k = jnp.where(live, k, 0.0)
    hi, mid, lo = _split3(logf)
    mats = mats_ref[...]
    e_all = _dot(mats, hi) + _dot(mats, mid) + _dot(mats, lo)
    b = e_all[:chunk]
    e_end = e_all[chunk:2 * chunk]
    b_end = b[chunk - 1:chunk, :]
    q = zq * (HG_DK ** -0.5)
    rr = lax.broadcasted_iota(jnp.int32, (chunk, chunk), 0)
    cc = lax.broadcasted_iota(jnp.int32, (chunk, chunk), 1)
    gon = gon_ref[...]

    for h in range(HG_HEADS):
        hs = slice(h * HG_DK, (h + 1) * HG_DK)
        qh, kh, vh, bh = q[:, hs], k[:, hs], zi[:, hs], b[:, hs]
        st = st_ref[h]
        o = _dot_nt((qh * jnp.exp(bh)).astype(BF16), st.astype(BF16))
        if levels:
            att = jnp.zeros((chunk, chunk), F32)
            for li, m in enumerate(levels):
                w = jnp.exp(e_all[(2 + li) * chunk:(3 + li) * chunk, hs])
                second = (row % (2 * m)) >= m
                qm = jnp.where(second, qh * w, 0.0).astype(BF16)
                km = jnp.where(second, 0.0, kh * w).astype(BF16)
                a = _dot_nt(qm, km)
                att = att + jnp.where((rr // (2 * m)) == (cc // (2 * m)), a, 0.0)
            o = o + _dot(att.astype(BF16), vh.astype(BF16))
        for dlt in range(min(HG_DIAG, chunk)):
            if dlt == 0:
                a = jnp.sum(qh * kh, axis=-1, keepdims=True)
                o = o + a * vh
            else:
                bs = pltpu.roll(bh, dlt, 0)
                ks = pltpu.roll(kh, dlt, 0)
                vs = pltpu.roll(vh, dlt, 0)
                w = jnp.where((row % HG_DIAG) >= dlt, jnp.exp(bh - bs), 0.0)
                a = jnp.sum(qh * ks * w, axis=-1, keepdims=True)
                o = o + a * vs
        on = _rms(o, gon)
        zgh = zg[:, hs]
        o_ref[0, :, hs] = (on * (zgh * _sigmoid(zgh))).astype(o_ref.dtype)
        ke = (kh * jnp.exp(e_end[:, hs])).astype(BF16)
        st_ref[h] = st * jnp.exp(b_end[:, hs]) + _dot_tn(vh.astype(BF16), ke)

    @pl.when(ci == pl.num_programs(1) - 1)
    def _():
        for h in range(HG_HEADS):
            sout_ref[0, h] = st_ref[h].T


def _hgrn(zh, lb, g_onorm, s0, chunk, t_valid):
    n, t, w = zh.shape
    mats = jnp.asarray(_hgrn_mats(chunk), BF16)
    kern = functools.partial(_hgrn_kernel, chunk=chunk, t_valid=t_valid)
    vw = HG_HEADS * HG_DV
    return pl.pallas_call(
        kern,
        out_shape=(jax.ShapeDtypeStruct((n, t, vw), BF16),
                   jax.ShapeDtypeStruct(s0.shape, F32)),
        grid=(n, t // chunk),
        in_specs=[
            pl.BlockSpec((1, chunk, w), lambda i, c: (i, c, 0)),
            pl.BlockSpec(mats.shape, lambda i, c: (0, 0)),
            pl.BlockSpec(lb.shape, lambda i, c: (0, 0)),
            pl.BlockSpec(g_onorm.shape, lambda i, c: (0, 0)),
            pl.BlockSpec((1,) + s0.shape[1:], lambda i, c: (i, 0, 0, 0)),
        ],
        out_specs=(
            pl.BlockSpec((1, chunk, vw), lambda i, c: (i, c, 0)),
            pl.BlockSpec((1,) + s0.shape[1:], lambda i, c: (i, 0, 0, 0)),
        ),
        scratch_shapes=[pltpu.VMEM((HG_HEADS, HG_DV, HG_DK), F32)],
        compiler_params=_cparams(("arbitrary", "arbitrary")),
        name="hgrn2",
    )(zh, mats, lb, g_onorm, s0)


def _softmax_step(s, v, m_ref, l_ref, acc_ref):
    m_old = m_ref[...]
    m_new = jnp.maximum(m_old, jnp.max(s, axis=-1, keepdims=True))
    p = jnp.exp(s - m_new)
    alpha = jnp.exp(m_old - m_new)
    l_ref[...] = alpha * l_ref[...] + jnp.sum(p, axis=-1, keepdims=True)
    acc_ref[...] = alpha * acc_ref[...] + _dot(p.astype(BF16), v)
    m_ref[...] = m_new


def _attn_prompt_kernel(qlat_ref, qpe_ref, kct_ref, kpet_ref, v_ref, wuv_ref, o_ref,
                        m_ref, l_ref, acc_ref, *, tk):
    i = pl.program_id(0)
    rows = MLA_HEADS * Q_TILE
    q1 = qlat_ref[0].reshape(rows, MLA_KV_RANK)
    q2 = qpe_ref[0].reshape(rows, MLA_ROPE)
    m_ref[...] = jnp.full(m_ref.shape, NEG_INF, F32)
    l_ref[...] = jnp.zeros(l_ref.shape, F32)
    acc_ref[...] = jnp.zeros(acc_ref.shape, F32)

    def logits(j):
        off = pl.multiple_of(j * tk, tk)
        s = _dot(q1, kct_ref[:, pl.ds(off, tk)]) + _dot(q2, kpet_ref[:, pl.ds(off, tk)])
        return s, v_ref[pl.ds(off, tk), :], off

    n_full = (i * Q_TILE) // tk

    def body(j, carry):
        s, v, _ = logits(j)
        _softmax_step(s, v, m_ref, l_ref, acc_ref)
        return carry

    lax.fori_loop(0, n_full, body, 0)
    s, v, off = logits(n_full)
    q_pos = i * Q_TILE + lax.broadcasted_iota(jnp.int32, (rows, tk), 0) % Q_TILE
    k_pos = off + lax.broadcasted_iota(jnp.int32, (rows, tk), 1)
    _softmax_step(jnp.where(k_pos <= q_pos, s, NEG_INF), v, m_ref, l_ref, acc_ref)

    o_lat = (acc_ref[...] / l_ref[...]).astype(BF16)
    for h in range(MLA_HEADS):
        o_ref[:, h * MLA_V:(h + 1) * MLA_V] = _dot(
            o_lat[h * Q_TILE:(h + 1) * Q_TILE], wuv_ref[h]).astype(o_ref.dtype)


def _attn_prompt(qlat, qpe, kct, kpet, kc, w_uv, tk):
    nq = qlat.shape[0]
    t = kc.shape[0]
    rows = MLA_HEADS * Q_TILE
    kern = functools.partial(_attn_prompt_kernel, tk=tk)
    return pl.pallas_call(
        kern,
        out_shape=jax.ShapeDtypeStruct((nq * Q_TILE, MLA_HEADS * MLA_V), BF16),
        grid=(nq,),
        in_specs=[
            pl.BlockSpec((1, MLA_HEADS, Q_TILE, MLA_KV_RANK), lambda i: (i, 0, 0, 0)),
            pl.BlockSpec((1, MLA_HEADS, Q_TILE, MLA_ROPE), lambda i: (i, 0, 0, 0)),
            pl.BlockSpec((MLA_KV_RANK, t), lambda i: (0, 0)),
            pl.BlockSpec((MLA_ROPE, t), lambda i: (0, 0)),
            pl.BlockSpec((t, MLA_KV_RANK), lambda i: (0, 0)),
            pl.BlockSpec(w_uv.shape, lambda i: (0, 0, 0)),
        ],
        out_specs=pl.BlockSpec((Q_TILE, MLA_HEADS * MLA_V), lambda i: (i, 0)),
        scratch_shapes=[pltpu.VMEM((rows, 1), F32), pltpu.VMEM((rows, 1), F32),
                        pltpu.VMEM((rows, MLA_KV_RANK), F32)],
        compiler_params=_cparams(("parallel",)),
        name="attn_prompt",
    )(qlat, qpe, kct, kpet, kc, w_uv)


PAGES_PER_STEP = 8
NEW_PAD = 16


def _attn_sample_kernel(pt_ref, q1_ref, q2_ref, *refs, t_new):
    npg = PAGES_PER_STEP
    c_refs = refs[:npg]
    p_refs = refs[npg:2 * npg]
    cn_ref, kn_ref, wuv_ref, o_ref, m_ref, l_ref, acc_ref = refs[2 * npg:]
    g = pl.program_id(1)

    @pl.when(g == 0)
    def _():
        m_ref[...] = jnp.full(m_ref.shape, NEG_INF, F32)
        l_ref[...] = jnp.zeros(l_ref.shape, F32)
        acc_ref[...] = jnp.zeros(acc_ref.shape, F32)

    q1 = q1_ref[0]
    q2 = q2_ref[0]
    c = jnp.concatenate([r[...].astype(BF16) for r in c_refs], axis=0)
    kp = jnp.concatenate([r[...].astype(BF16) for r in p_refs], axis=0)
    s = _dot_nt(q1, c) + _dot_nt(q2, kp)
    _softmax_step(s, c, m_ref, l_ref, acc_ref)

    @pl.when(g == pl.num_programs(1) - 1)
    def _():
        cn = cn_ref[0]
        rows = q1.shape[0]
        s2 = _dot_nt(q1, cn) + _dot_nt(q2, kn_ref[0])
        t_q = lax.broadcasted_iota(jnp.int32, (rows, NEW_PAD), 0) % t_new
        t_k = lax.broadcasted_iota(jnp.int32, (rows, NEW_PAD), 1)
        _softmax_step(jnp.where(t_k <= t_q, s2, NEG_INF), cn, m_ref, l_ref, acc_ref)
        o_lat = (acc_ref[...] / l_ref[...]).astype(BF16)
        for h in range(MLA_HEADS):
            o_ref[0, :, h * MLA_V:(h + 1) * MLA_V] = _dot(
                o_lat[h * t_new:(h + 1) * t_new], wuv_ref[h]).astype(o_ref.dtype)


def _attn_sample(page_table, q1, q2, cache_ckv, cache_kpe, cn, kn, w_uv, t_new):
    n, rows, _ = q1.shape
    n_pages = page_table.shape[1]
    page = cache_ckv.shape[2]
    npg = PAGES_PER_STEP

    def page_spec(width, k):
        return pl.BlockSpec((None, None, page, width),
                            lambda i, g, pt: (0, pt[i, g * npg + k], 0, 0))

    in_specs = [
        pl.BlockSpec((1, rows, MLA_KV_RANK), lambda i, g, pt: (i, 0, 0)),
        pl.BlockSpec((1, rows, MLA_ROPE), lambda i, g, pt: (i, 0, 0)),
    ]
    in_specs += [page_spec(MLA_KV_RANK, k) for k in range(npg)]
    in_specs += [page_spec(MLA_ROPE, k) for k in range(npg)]
    in_specs += [
        pl.BlockSpec((1, NEW_PAD, MLA_KV_RANK), lambda i, g, pt: (i, 0, 0)),
        pl.BlockSpec((1, NEW_PAD, MLA_ROPE), lambda i, g, pt: (i, 0, 0)),
        pl.BlockSpec(w_uv.shape, lambda i, g, pt: (0, 0, 0)),
    ]
    kern = functools.partial(_attn_sample_kernel, t_new=t_new)
    return pl.pallas_call(
        kern,
        out_shape=jax.ShapeDtypeStruct((n, t_new, MLA_HEADS * MLA_V), BF16),
        grid_spec=pltpu.PrefetchScalarGridSpec(
            num_scalar_prefetch=1,
            grid=(n, n_pages // npg),
            in_specs=in_specs,
            out_specs=pl.BlockSpec((1, t_new, MLA_HEADS * MLA_V), lambda i, g, pt: (i, 0, 0)),
            scratch_shapes=[pltpu.VMEM((rows, 1), F32), pltpu.VMEM((rows, 1), F32),
                            pltpu.VMEM((rows, MLA_KV_RANK), F32)],
        ),
        compiler_params=_cparams(("parallel", "arbitrary")),
        name="attn_sample",
    )(page_table, q1, q2, *([cache_ckv] * npg), *([cache_kpe] * npg), cn, kn, w_uv)


def _merge_kernel(oa_ref, ob_ref, gates_ref, x_ref, wpa_ref, wpb_ref, wout_ref, gffn_ref,
                  wr_ref, br_ref, x1_ref, h2_ref, ti_ref, tg_ref):
    d = x_ref.shape[1]
    gates = gates_ref[...]
    mix = gates[:, :d] * _dot(oa_ref[...], wpa_ref[...]) + gates[:, d:] * _dot(ob_ref[...], wpb_ref[...])
    x1 = x_ref[...] + _dot(mix.astype(BF16), wout_ref[...])
    x1_ref[...] = x1
    h2 = _rms(x1, gffn_ref[...])
    h2_ref[...] = h2
    h_hi, h_mid, h_lo = _split3(h2)
    w_hi, w_mid, w_lo = wr_ref[0], wr_ref[1], wr_ref[2]
    logits = (_dot(h_hi, w_hi) + (_dot(h_hi, w_mid) + _dot(h_mid, w_hi))
              + (_dot(h_mid, w_mid) + _dot(h_hi, w_lo) + _dot(h_lo, w_hi))) + br_ref[...]
    lane = lax.broadcasted_iota(jnp.int32, logits.shape, 1)
    vals, idxs = [], []
    cur = logits
    for _ in range(TOP_K):
        mx = jnp.max(cur, axis=-1, keepdims=True)
        ix = jnp.min(jnp.where(cur == mx, lane, LANES), axis=-1, keepdims=True)
        vals.append(mx)
        idxs.append(ix)
        cur = jnp.where(lane == ix, NEG_INF, cur)
    es = [jnp.exp(v - vals[0]) for v in vals]
    tot = es[0] + es[1] + es[2] + es[3]
    ti = jnp.zeros(logits.shape, jnp.int32)
    tg = jnp.zeros(logits.shape, F32)
    for kk in range(TOP_K):
        ti = jnp.where(lane == kk, idxs[kk], ti)
        tg = jnp.where(lane == kk, es[kk] / tot, tg)
    ti_ref[...] = ti
    tg_ref[...] = tg


def _merge(o_a, o_b, gates, x_all, w_pa, w_pb, w_out, g_ffn, w_r3, b_r, tm):
    n, d = x_all.shape
    row = lambda i: (i, 0)
    full = lambda a: pl.BlockSpec(a.shape, (lambda i: (0, 0)) if a.ndim == 2 else (lambda i: (0, 0, 0)))
    return pl.pallas_call(
        _merge_kernel,
        out_shape=(jax.ShapeDtypeStruct((n, d), F32), jax.ShapeDtypeStruct((n, d), F32),
                   jax.ShapeDtypeStruct((n, LANES), jnp.int32), jax.ShapeDtypeStruct((n, LANES), F32)),
        grid=(n // tm,),
        in_specs=[pl.BlockSpec((tm, o_a.shape[1]), row), pl.BlockSpec((tm, o_b.shape[1]), row),
                  pl.BlockSpec((tm, gates.shape[1]), row), pl.BlockSpec((tm, d), row),
                  full(w_pa), full(w_pb), full(w_out), full(g_ffn), full(w_r3), full(b_r)],
        out_specs=(pl.BlockSpec((tm, d), row), pl.BlockSpec((tm, d), row),
                   pl.BlockSpec((tm, LANES), row), pl.BlockSpec((tm, LANES), row)),
        compiler_params=_cparams(("parallel",)),
        name="merge_router",
    )(o_a, o_b, gates, x_all, w_pa, w_pb, w_out, g_ffn, w_r3, b_r)


def _moe_kernel(blke_ref, nreal_ref, rowtok_ref, h2_hbm, w1g_ref, w1l_ref, b1g_ref, b1l_ref,
                w2_ref, b2_ref, out_ref, xbuf, sem):
    b = pl.program_id(0)
    n_real = nreal_ref[0]
    rows = xbuf.shape[1]

    def issue(blk, slot):
        def body(r, carry):
            tok = rowtok_ref[blk * rows + r]
            pltpu.make_async_copy(h2_hbm.at[pl.ds(tok, 1), :], xbuf.at[slot, pl.ds(r, 1), :],
                                  sem.at[slot]).start()
            return carry
        lax.fori_loop(0, rows, body, 0)

    @pl.when(b == 0)
    def _():
        issue(0, 0)

    @pl.when(b + 1 < n_real)
    def _():
        issue(b + 1, (b + 1) % 2)

    @pl.when(b < n_real)
    def _():
        slot = b % 2
        pltpu.make_async_copy(h2_hbm.at[pl.ds(0, rows), :], xbuf.at[slot], sem.at[slot]).wait()
        x = xbuf[slot].astype(BF16)
        glu = jnp.minimum(_dot(x, w1g_ref[...]) + b1g_ref[...], SWIGLU_LIMIT)
        lin = jnp.clip(_dot(x, w1l_ref[...]) + b1l_ref[...], -SWIGLU_LIMIT, SWIGLU_LIMIT)
        act = glu * _sigmoid(SWIGLU_ALPHA * glu) * (lin + 1.0)
        out_ref[...] = _dot(act.astype(BF16), w2_ref[...]) + b2_ref[...]

    @pl.when(b >= n_real)
    def _():
        out_ref[...] = jnp.zeros(out_ref.shape, out_ref.dtype)


def _moe(blk_e, n_real, row_tok, h2, w1g, w1l, b1g, b1l, w2, b2):
    n_blocks = blk_e.shape[0]
    d = h2.shape[1]
    dff = w1g.shape[2]
    wspec = lambda a: pl.BlockSpec((None,) + a.shape[1:], lambda b, be, nr, rt: (be[b], 0, 0))
    return pl.pallas_call(
        _moe_kernel,
        out_shape=jax.ShapeDtypeStruct((n_blocks * MOE_ROWS, d), F32),
        grid_spec=pltpu.PrefetchScalarGridSpec(
            num_scalar_prefetch=3,
            grid=(n_blocks,),
            in_specs=[pl.BlockSpec(memory_space=pl.ANY),
                      wspec(w1g), wspec(w1l), wspec(b1g), wspec(b1l), wspec(w2), wspec(b2)],
            out_specs=pl.BlockSpec((MOE_ROWS, d), lambda b, be, nr, rt: (b, 0)),
            scratch_shapes=[pltpu.VMEM((2, MOE_ROWS, d), F32), pltpu.SemaphoreType.DMA((2,))],
        ),
        compiler_params=_cparams(("arbitrary",)),
        name="moe_experts",
    )(blk_e, n_real, row_tok, h2, w1g, w1l, b1g, b1l, w2, b2)


def _combine_kernel(pos_ref, ys_hbm, x1_ref, tg_ref, p_ref, wple_ref, wpg_ref, gple_ref, gfin_ref,
                    y_ref, buf, sem):
    i = pl.program_id(0)
    tm = x1_ref.shape[0]

    def issue(blk, slot):
        def body(r, carry):
            for kk in range(TOP_K):
                src = pos_ref[(blk * tm + r) * TOP_K + kk]
                pltpu.make_async_copy(ys_hbm.at[pl.ds(src, 1), :], buf.at[slot, kk, pl.ds(r, 1), :],
                                      sem.at[slot]).start()
            return carry
        lax.fori_loop(0, tm, body, 0)

    @pl.when(i == 0)
    def _():
        issue(0, 0)

    @pl.when(i + 1 < pl.num_programs(0))
    def _():
        issue(i + 1, (i + 1) % 2)

    slot = i % 2
    for kk in range(TOP_K):
        pltpu.make_async_copy(ys_hbm.at[pl.ds(0, tm), :], buf.at[slot, kk], sem.at[slot]).wait()
    tg = tg_ref[...]
    x2 = x1_ref[...]
    for kk in range(TOP_K):
        x2 = x2 + tg[:, kk:kk + 1] * buf[slot, kk]
    gate = _sigmoid(_dot(_rms(x2, gple_ref[...]).astype(BF16), wpg_ref[...]))
    x3 = x2 + _dot(p_ref[...].astype(BF16), wple_ref[...]) * gate
    y_ref[...] = _rms(x3, gfin_ref[...])


def _combine(pos, ys, x1, tg, p_all, w_ple, w_pg, g_ple, g_fin, tm):
    n, d = x1.shape
    row = lambda i, ps: (i, 0)
    full = lambda a: pl.BlockSpec(a.shape, lambda i, ps: (0, 0))
    return pl.pallas_call(
        _combine_kernel,
        out_shape=jax.ShapeDtypeStruct((n, d), F32),
        grid_spec=pltpu.PrefetchScalarGridSpec(
            num_scalar_prefetch=1,
            grid=(n // tm,),
            in_specs=[pl.BlockSpec(memory_space=pl.ANY),
                      pl.BlockSpec((tm, d), row), pl.BlockSpec((tm, LANES), row),
                      pl.BlockSpec((tm, p_all.shape[1]), row),
                      full(w_ple), full(w_pg), full(g_ple), full(g_fin)],
            out_specs=pl.BlockSpec((tm, d), row),
            scratch_shapes=[pltpu.VMEM((2, TOP_K, tm, d), F32), pltpu.SemaphoreType.DMA((2,))],
        ),
        compiler_params=_cparams(("arbitrary",)),
        name="combine_ple",
    )(pos, ys, x1, tg, p_all, w_ple, w_pg, g_ple, g_fin)


def _routing(top_i, n_tok):
    a = n_tok * TOP_K
    flat_e = top_i.reshape(a)
    onehot = (flat_e[:, None] == jnp.arange(N_EXPERTS, dtype=jnp.int32)[None, :]).astype(jnp.int32)
    rank = jnp.take_along_axis(jnp.cumsum(onehot, axis=0), flat_e[:, None], axis=1)[:, 0] - 1
    counts = jnp.sum(onehot, axis=0)
    padded = (counts + MOE_ROWS - 1) // MOE_ROWS * MOE_ROWS
    pad_end = jnp.cumsum(padded)
    pad_start = pad_end - padded
    dest = (pad_start[flat_e] + rank).astype(jnp.int32)
    n_blocks = -(-a // MOE_ROWS) + N_EXPERTS
    row_tok = jnp.zeros((n_blocks * MOE_ROWS,), jnp.int32).at[dest].set(
        jnp.arange(a, dtype=jnp.int32) // TOP_K)
    blk_e = jnp.minimum(
        jnp.searchsorted(pad_end, jnp.arange(n_blocks, dtype=jnp.int32) * MOE_ROWS, side="right"),
        N_EXPERTS - 1).astype(jnp.int32)
    n_real = (pad_end[-1] // MOE_ROWS).astype(jnp.int32).reshape(1)
    return dest, row_tok, blk_e, n_real


def _pick_tile(n, prefs):
    for t in prefs:
        if n % t == 0:
            return t
    raise ValueError(f"no tile in {prefs} divides {n}")


def kernel(x_prompt, x_sample, cache_ckv, cache_kpe, state_hgrn, page_table, p_prompt, p_sample, hg_lb, g_mix, w_in, g_qnorm, w_uq, w_uk, w_uv, g_kvnorm, g_onorm, w_pa, w_pb, w_out, g_ffn, w_router, b_router, w1, b1, w2, b2, g_ple, w_ple, w_pg, g_final):
    n_p, t_p, d = x_prompt.shape
    n_s, t_s, _ = x_sample.shape
    depth = w_in.shape[0]
    assert depth == 1 and n_p == 1
    n_pages = page_table.shape[1]
    page = cache_ckv.shape[2]
    past = n_pages * page
    tok_p = n_p * t_p
    tok_s = n_s * t_s
    n_tok = tok_p + tok_s
    assert tok_p % Q_TILE == 0 and tok_s % Q_TILE == 0 and n_pages % PAGES_PER_STEP == 0
    qw = HG_HEADS * HG_DK
    vw = HG_HEADS * HG_DV
    row2 = lambda v: v.reshape(1, -1).astype(F32)

    lb = jnp.cumsum(jax.nn.softmax(hg_lb.astype(F32), axis=0), axis=0)[0].reshape(1, qw)
    wi = w_in[0]
    hg_cols = 2 * qw + 2 * vw
    mla_cols = MLA_Q_RANK + MLA_KV_RANK + MLA_ROPE
    w_h = wi[:, :hg_cols].astype(BF16)
    w_mla = wi[:, hg_cols:hg_cols + mla_cols].astype(BF16)
    w_g = wi[:, hg_cols + mla_cols:].astype(BF16)
    wq = w_uq[0].reshape(MLA_Q_RANK, MLA_HEADS, MLA_NOPE + MLA_ROPE)
    w_uq_p = jnp.concatenate([wq[:, :, :MLA_NOPE].reshape(MLA_Q_RANK, -1),
                              wq[:, :, MLA_NOPE:].reshape(MLA_Q_RANK, -1)], axis=1).astype(BF16)
    w_ukt = jnp.transpose(w_uk[0], (1, 2, 0)).astype(BF16)
    w_uvh = jnp.transpose(w_uv[0], (1, 0, 2)).astype(BF16)
    w_r = jnp.zeros((d, LANES), F32).at[:, :N_EXPERTS].set(w_router[0].astype(F32))
    w_r3 = jnp.stack(_split3(w_r))
    b_r = jnp.full((1, LANES), -1e30, F32).at[0, :N_EXPERTS].set(b_router[0].astype(F32))
    w1g = w1[0, :, :, 0::2].astype(BF16)
    w1l = w1[0, :, :, 1::2].astype(BF16)
    b1g = b1[0, :, None, 0::2].astype(F32)
    b1l = b1[0, :, None, 1::2].astype(F32)
    w2b = w2[0].astype(BF16)
    b2r = b2[0, :, None, :].astype(F32)

    half = MLA_ROPE // 2
    inv = ROPE_THETA ** (-jnp.arange(half, dtype=F32) / half)
    pos = jnp.concatenate([jnp.tile(jnp.arange(t_p), n_p), jnp.tile(past + jnp.arange(t_s), n_s)])
    ang = pos.astype(F32)[:, None] * inv[None, :]
    cos2 = jnp.concatenate([jnp.cos(ang), jnp.cos(ang)], axis=1)
    sin2 = jnp.concatenate([-jnp.sin(ang), jnp.sin(ang)], axis=1)

    x_all = jnp.concatenate([x_prompt.reshape(tok_p, d), x_sample.reshape(tok_s, d)], axis=0)
    p_all = jnp.concatenate([p_prompt[0].reshape(tok_p, -1), p_sample[0].reshape(tok_s, -1)], axis=0)

    tm = _pick_tile(n_tok, (512, 256, 128))
    zh, gates, c_new, kpe_new, qlat, qpe, kcb, kpeb = _inproj(
        x_all, cos2, sin2, row2(g_mix[0]), w_h, w_mla, w_g, row2(g_qnorm[0]), w_uq_p, w_ukt,
        row2(g_kvnorm[0]), tm)

    gon = row2(g_onorm[0])
    chunk_p = 64 if t_p % 64 == 0 else t_p
    oa_p, st_p = _hgrn(zh[:tok_p].reshape(n_p, t_p, -1), lb, gon,
                       jnp.zeros((n_p, HG_HEADS, HG_DK, HG_DV), F32), chunk_p, chunk_p)
    t_pad = -(-t_s // SUBLANES) * SUBLANES
    zh_s = jnp.pad(zh[tok_p:].reshape(n_s, t_s, -1), ((0, 0), (0, t_pad - t_s), (0, 0)))
    oa_s, st_s = _hgrn(zh_s, lb, gon, state_hgrn[0].astype(F32), t_pad, t_s)
    o_a = jnp.concatenate([oa_p.reshape(tok_p, vw), oa_s[:, :t_s].reshape(tok_s, vw)], axis=0)

    nq_p = tok_p // Q_TILE
    tk = _pick_tile(tok_p, (256, 128))
    ob_p = _attn_prompt(qlat[:nq_p], qpe[:nq_p], kcb[:tok_p].T, kpeb[:tok_p].T, kcb[:tok_p], w_uvh, tk)

    def sample_rows(a):
        w = a.shape[-1]
        a = a.reshape(-1, MLA_HEADS, Q_TILE // t_s, t_s, w)
        return jnp.transpose(a, (0, 2, 1, 3, 4)).reshape(n_s, MLA_HEADS * t_s, w)

    def new_keys(a):
        return jnp.pad(a.reshape(n_s, t_s, -1), ((0, 0), (0, NEW_PAD - t_s), (0, 0)))

    ob_s = _attn_sample(page_table, sample_rows(qlat[nq_p:]), sample_rows(qpe[nq_p:]),
                        cache_ckv, cache_kpe, new_keys(kcb[tok_p:]), new_keys(kpeb[tok_p:]), w_uvh, t_s)
    o_b = jnp.concatenate([ob_p, ob_s.reshape(tok_s, -1)], axis=0)

    x1, h2, top_i, top_g = _merge(o_a, o_b, gates, x_all, w_pa[0].astype(BF16), w_pb[0].astype(BF16),
                                  w_out[0].astype(BF16), row2(g_ffn[0]), w_r3, b_r, tm)

    dest, row_tok, blk_e, n_real = _routing(top_i[:, :TOP_K], n_tok)
    ys = _moe(blk_e, n_real, row_tok, h2, w1g, w1l, b1g, b1l, w2b, b2r)

    tc = _pick_tile(n_tok, (256, 128))
    y_all = _combine(dest, ys, x1, top_g, p_all, w_ple[0].astype(BF16), w_pg[0].astype(BF16),
                     row2(g_ple[0]), row2(g_final), tc)

    y_prompt = y_all[:tok_p].reshape(n_p, t_p, d)
    y_sample = y_all[tok_p:].reshape(n_s, t_s, d)
    return (y_prompt, y_sample,
            c_new[:tok_p].reshape(1, n_p, t_p, -1), kpe_new[:tok_p].reshape(1, n_p, t_p, -1), st_p[None],
            c_new[tok_p:].reshape(1, n_s, t_s, -1), kpe_new[tok_p:].reshape(1, n_s, t_s, -1), st_s[None])
```

```python
import functools
import math

import numpy as np
import jax
import jax.numpy as jnp
from jax import lax
from jax.experimental import pallas as pl
from jax.experimental.pallas import tpu as pltpu

F32 = jnp.float32
BF16 = jnp.bfloat16

HG_HEADS = 4
HG_DK = 128
HG_DV = 128
MLA_HEADS = 4
MLA_NOPE = 128
MLA_ROPE = 64
MLA_V = 128
MLA_Q_RANK = 384
MLA_KV_RANK = 256
MLA_SCALE = (MLA_NOPE + MLA_ROPE) ** -0.5
Q_SCALE = MLA_SCALE * math.log2(math.e)
ROPE_THETA = 10000.0
N_EXPERTS = 32
TOP_K = 4
SWIGLU_ALPHA = 1.702
SWIGLU_LIMIT = 7.0
EPS = 1e-6

LANES = 128
SUBLANES = 8
VMEM_LIMIT = 56 * 1024 * 1024

Q_TILE = 128
MOE_ROWS = 256
NEG_INF = float("-inf")


def _cparams(sem):
    return pltpu.CompilerParams(dimension_semantics=sem, vmem_limit_bytes=VMEM_LIMIT)


def _dot(a, b):
    return jnp.dot(a, b, preferred_element_type=F32)


def _dot_nt(a, b):
    return lax.dot_general(a, b, (((1,), (1,)), ((), ())), preferred_element_type=F32)


def _dot_tn(a, b):
    return lax.dot_general(a, b, (((0,), (0,)), ((), ())), preferred_element_type=F32)


def _rms(x, g):
    return x * lax.rsqrt(jnp.mean(x * x, axis=-1, keepdims=True) + EPS) * g


def _sigmoid(x):
    return 1.0 / (1.0 + jnp.exp(-x))


def _split3(x):
    hi = x.astype(BF16)
    r1 = x - hi.astype(F32)
    mid = r1.astype(BF16)
    lo = (r1 - mid.astype(F32)).astype(BF16)
    return hi, mid, lo


def _rope64(v, cos2, sin2):
    half = MLA_ROPE // 2
    partner = jnp.concatenate([v[:, half:], v[:, :half]], axis=1)
    return v * cos2 + partner * sin2


def _inproj_kernel(x_ref, cos_ref, sin_ref, cost_ref, sint_ref, gmix_ref, wh_ref, wmla_ref, wg_ref,
                   gq_ref, wuqt_ref, wuk_ref, gkv_ref,
                   zh_ref, gates_ref, c_ref, kpe_ref, qlat_ref, qpe_ref, kcb_ref, kpeb_ref):
    tm = x_ref.shape[0]
    half = MLA_ROPE // 2
    h = _rms(x_ref[...], gmix_ref[...]).astype(BF16)
    zh_ref[...] = _dot(h, wh_ref[...])
    gates_ref[...] = _sigmoid(_dot(h, wg_ref[...]))
    zm = _dot(h, wmla_ref[...])
    c = _rms(zm[:, MLA_Q_RANK:MLA_Q_RANK + MLA_KV_RANK], gkv_ref[...])
    c_ref[...] = c
    kcb_ref[...] = c.astype(BF16)
    kpe = _rope64(zm[:, MLA_Q_RANK + MLA_KV_RANK:], cos_ref[...], sin_ref[...])
    kpe_ref[...] = kpe
    kpeb_ref[...] = kpe.astype(BF16)
    qn_t = _rms(zm[:, :MLA_Q_RANK], gq_ref[...]).T.astype(BF16)
    q_t = _dot(wuqt_ref[...], qn_t)
    cos_t = cost_ref[...]
    sin_t = sint_ref[...]
    for hh in range(MLA_HEADS):
        q_nope = q_t[hh * MLA_NOPE:(hh + 1) * MLA_NOPE].astype(BF16)
        qlat = (_dot(wuk_ref[hh], q_nope) * Q_SCALE).astype(BF16)
        off = MLA_HEADS * MLA_NOPE + hh * MLA_ROPE
        x1 = q_t[off:off + half]
        x2 = q_t[off + half:off + MLA_ROPE]
        qpe = (jnp.concatenate([x1 * cos_t - x2 * sin_t, x1 * sin_t + x2 * cos_t], axis=0)
               * Q_SCALE).astype(BF16)
        for tb in range(tm // Q_TILE):
            qlat_ref[tb, :, hh * Q_TILE:(hh + 1) * Q_TILE] = qlat[:, tb * Q_TILE:(tb + 1) * Q_TILE]
            qpe_ref[tb, :, hh * Q_TILE:(hh + 1) * Q_TILE] = qpe[:, tb * Q_TILE:(tb + 1) * Q_TILE]


def _inproj(x_all, cos2, sin2, cos_t, sin_t, g_mix, w_h, w_mla, w_g, g_q, w_uqt, w_ukh, g_kv, tm):
    n, d = x_all.shape
    nq = n // Q_TILE
    half = MLA_ROPE // 2
    rows = MLA_HEADS * Q_TILE
    row = lambda i: (i, 0)
    col = lambda i: (0, i)
    const2 = lambda i: (0, 0)
    const3 = lambda i: (0, 0, 0)
    full = lambda a: pl.BlockSpec(a.shape, const2 if a.ndim == 2 else const3)
    out_shape = (
        jax.ShapeDtypeStruct((n, w_h.shape[1]), F32),
        jax.ShapeDtypeStruct((n, w_g.shape[1]), F32),
        jax.ShapeDtypeStruct((n, MLA_KV_RANK), F32),
        jax.ShapeDtypeStruct((n, MLA_ROPE), F32),
        jax.ShapeDtypeStruct((nq, MLA_KV_RANK, rows), BF16),
        jax.ShapeDtypeStruct((nq, MLA_ROPE, rows), BF16),
        jax.ShapeDtypeStruct((n, MLA_KV_RANK), BF16),
        jax.ShapeDtypeStruct((n, MLA_ROPE), BF16),
    )
    tq = tm // Q_TILE
    out_specs = (
        pl.BlockSpec((tm, w_h.shape[1]), row),
        pl.BlockSpec((tm, w_g.shape[1]), row),
        pl.BlockSpec((tm, MLA_KV_RANK), row),
        pl.BlockSpec((tm, MLA_ROPE), row),
        pl.BlockSpec((tq, MLA_KV_RANK, rows), lambda i: (i, 0, 0)),
        pl.BlockSpec((tq, MLA_ROPE, rows), lambda i: (i, 0, 0)),
        pl.BlockSpec((tm, MLA_KV_RANK), row),
        pl.BlockSpec((tm, MLA_ROPE), row),
    )
    in_specs = [
        pl.BlockSpec((tm, d), row),
        pl.BlockSpec((tm, MLA_ROPE), row),
        pl.BlockSpec((tm, MLA_ROPE), row),
        pl.BlockSpec((half, tm), col),
        pl.BlockSpec((half, tm), col),
        full(g_mix), full(w_h), full(w_mla), full(w_g), full(g_q), full(w_uqt), full(w_ukh), full(g_kv),
    ]
    return pl.pallas_call(
        _inproj_kernel,
        out_shape=out_shape,
        grid=(n // tm,),
        in_specs=in_specs,
        out_specs=out_specs,
        compiler_params=_cparams(("parallel",)),
        name="inproj",
    )(x_all, cos2, sin2, cos_t, sin_t, g_mix, w_h, w_mla, w_g, g_q, w_uqt, w_ukh, g_kv)


HG_DIAG = 8


def _hgrn_levels(chunk):
    levels, m = [], chunk // 2
    while m >= HG_DIAG:
        levels.append(m)
        m //= 2
    return tuple(levels)


def _hgrn_mats(chunk):
    r = np.arange(chunk)[:, None]
    j = np.arange(chunk)[None, :]
    mats = [j <= r, j > r]
    for m in _hgrn_levels(chunk):
        same = (r // (2 * m)) == (j // (2 * m))
        second = (r % (2 * m)) >= m
        mid = (r // (2 * m)) * 2 * m + m
        a = second & same & (j >= mid) & (j <= r)
        b = (~second) & same & (j > r) & (j < mid)
        mats.append(a | b)
    return np.concatenate(mats, axis=0).astype(np.float32)


def _hgrn_kernel(zh_ref, mats_ref, lb_ref, gon_ref, s0_ref, o_ref, sout_ref, st_ref, *, chunk, t_valid):
    ci = pl.program_id(1)
    levels = _hgrn_levels(chunk)
    qw = HG_HEADS * HG_DK

    @pl.when(ci == 0)
    def _():
        for h in range(HG_HEADS):
            st_ref[h] = s0_ref[0, h].T

    zh = zh_ref[0]
    zq, zf, zi, zg = zh[:, :qw], zh[:, qw:2 * qw], zh[:, 2 * qw:3 * qw], zh[:, 3 * qw:]
    lb = lb_ref[...]
    f = lb + (1.0 - lb) * _sigmoid(zf)
    k = 1.0 - f
    logf = jnp.log(f)
    row = lax.broadcasted_iota(jnp.int32, (chunk, 1), 0)
    if t_valid < chunk:
        live = row < t_valid
        logf = jnp.where(live, logf, 0.0)
        k = jnp.where(live, k, 0.0)
    hi, mid, lo = _split3(logf)
    mats = mats_ref[...]
    e_all = _dot(mats, hi) + _dot(mats, mid) + _dot(mats, lo)
    b = e_all[:chunk]
    e_end = e_all[chunk:2 * chunk]
    b_end = b[chunk - 1:chunk, :]
    q = zq * (HG_DK ** -0.5)
    rr = lax.broadcasted_iota(jnp.int32, (chunk, chunk), 0)
    cc = lax.broadcasted_iota(jnp.int32, (chunk, chunk), 1)
    gon = gon_ref[...]

    for h in range(HG_HEADS):
        hs = slice(h * HG_DK, (h + 1) * HG_DK)
        qh, kh, vh, bh = q[:, hs], k[:, hs], zi[:, hs], b[:, hs]
        st = st_ref[h]
        o = _dot_nt((qh * jnp.exp(bh)).astype(BF16), st.astype(BF16))
        if levels:
            att = jnp.zeros((chunk, chunk), F32)
            for li, m in enumerate(levels):
                w = jnp.exp(e_all[(2 + li) * chunk:(3 + li) * chunk, hs])
                second = (row % (2 * m)) >= m
                qm = jnp.where(second, qh * w, 0.0).astype(BF16)
                km = jnp.where(second, 0.0, kh * w).astype(BF16)
                a = _dot_nt(qm, km)
                att = att + jnp.where((rr // (2 * m)) == (cc // (2 * m)), a, 0.0)
            o = o + _dot(att.astype(BF16), vh.astype(BF16))
        for dlt in range(min(HG_DIAG, chunk)):
            if dlt == 0:
                a = jnp.sum(qh * kh, axis=-1, keepdims=True)
                o = o + a * vh
            else:
                bs = pltpu.roll(bh, dlt, 0)
                ks = pltpu.roll(kh, dlt, 0)
                vs = pltpu.roll(vh, dlt, 0)
                w = jnp.where((row % HG_DIAG) >= dlt, jnp.exp(bh - bs), 0.0)
                a = jnp.sum(qh * ks * w, axis=-1, keepdims=True)
                o = o + a * vs
        on = _rms(o, gon)
        zgh = zg[:, hs]
        o_ref[0, :, hs] = (on * (zgh * _sigmoid(zgh))).astype(o_ref.dtype)
        ke = (kh * jnp.exp(e_end[:, hs])).astype(BF16)
        st_ref[h] = st * jnp.exp(b_end[:, hs]) + _dot_tn(vh.astype(BF16), ke)

    @pl.when(ci == pl.num_programs(1) - 1)
    def _():
        for h in range(HG_HEADS):
            sout_ref[0, h] = st_ref[h].T


def _hgrn(zh, lb, g_onorm, s0, chunk, t_valid):
    n, t, w = zh.shape
    mats = jnp.asarray(_hgrn_mats(chunk), BF16)
    kern = functools.partial(_hgrn_kernel, chunk=chunk, t_valid=t_valid)
    vw = HG_HEADS * HG_DV
    return pl.pallas_call(
        kern,
        out_shape=(jax.ShapeDtypeStruct((n, t, vw), BF16),
                   jax.ShapeDtypeStruct(s0.shape, F32)),
        grid=(n, t // chunk),
        in_specs=[
            pl.BlockSpec((1, chunk, w), lambda i, c: (i, c, 0)),
            pl.BlockSpec(mats.shape, lambda i, c: (0, 0)),
            pl.BlockSpec(lb.shape, lambda i, c: (0, 0)),
            pl.BlockSpec(g_onorm.shape, lambda i, c: (0, 0)),
            pl.BlockSpec((1,) + s0.shape[1:], lambda i, c: (i, 0, 0, 0)),
        ],
        out_specs=(
            pl.BlockSpec((1, chunk, vw), lambda i, c: (i, c, 0)),
            pl.BlockSpec((1,) + s0.shape[1:], lambda i, c: (i, 0, 0, 0)),
        ),
        scratch_shapes=[pltpu.VMEM((HG_HEADS, HG_DV, HG_DK), F32)],
        compiler_params=_cparams(("arbitrary", "arbitrary")),
        name="hgrn2",
    )(zh, mats, lb, g_onorm, s0)


ATTN_COLS = 256


def _attn_prompt_kernel(qlat_ref, qpe_ref, kc_ref, kpe_ref, kct_ref, wuvt_ref, o_ref,
                        m_ref, l_ref, acc_ref, *, tk):
    i = pl.program_id(0)
    rows = MLA_HEADS * Q_TILE
    m_ref[...] = jnp.full(m_ref.shape, NEG_INF, F32)
    l_ref[...] = jnp.zeros(l_ref.shape, F32)
    acc_ref[...] = jnp.zeros(acc_ref.shape, F32)

    def block(j, masked):
        off = pl.multiple_of(j * tk, tk)
        kc = kc_ref[pl.ds(off, tk), :]
        kp = kpe_ref[pl.ds(off, tk), :]
        kct = kct_ref[:, pl.ds(off, tk)]
        for c0 in range(0, rows, ATTN_COLS):
            cs = slice(c0, c0 + ATTN_COLS)
            s = _dot(kc, qlat_ref[0, :, cs]) + _dot(kp, qpe_ref[0, :, cs])
            if masked:
                k_pos = off + lax.broadcasted_iota(jnp.int32, s.shape, 0)
                q_pos = i * Q_TILE + lax.broadcasted_iota(jnp.int32, s.shape, 1) % Q_TILE
                s = jnp.where(k_pos <= q_pos, s, NEG_INF)
            m_old = m_ref[:, cs]
            m_new = jnp.maximum(m_old, jnp.max(s, axis=0, keepdims=True))
            p = jnp.exp2(s - m_new)
            alpha = jnp.exp2(m_old - m_new)
            l_ref[:, cs] = alpha * l_ref[:, cs] + jnp.sum(p, axis=0, keepdims=True)
            acc_ref[:, cs] = alpha * acc_ref[:, cs] + _dot(kct, p.astype(BF16))
            m_ref[:, cs] = m_new

    n_full = (i * Q_TILE) // tk

    def body(j, carry):
        block(j, False)
        return carry

    lax.fori_loop(0, n_full, body, 0)
    block(n_full, True)

    o_lat = (acc_ref[...] / l_ref[...]).astype(BF16)
    for h in range(MLA_HEADS):
        ob_t = _dot(wuvt_ref[h], o_lat[:, h * Q_TILE:(h + 1) * Q_TILE])
        o_ref[:, h * MLA_V:(h + 1) * MLA_V] = ob_t.T.astype(o_ref.dtype)


def _attn_prompt(qlat_t, qpe_t, kc, kpe, kct, w_uvt, tk):
    nq = qlat_t.shape[0]
    t = kc.shape[0]
    rows = MLA_HEADS * Q_TILE
    kern = functools.partial(_attn_prompt_kernel, tk=tk)
    resident = lambda shape: pl.BlockSpec(shape, lambda i: (0,) * len(shape), pipeline_mode=pl.Buffered(1))
    return pl.pallas_call(
        kern,
        out_shape=jax.ShapeDtypeStruct((nq * Q_TILE, MLA_HEADS * MLA_V), BF16),
        grid=(nq,),
        in_specs=[
            pl.BlockSpec((1, MLA_KV_RANK, rows), lambda i: (i, 0, 0)),
            pl.BlockSpec((1, MLA_ROPE, rows), lambda i: (i, 0, 0)),
            resident((t, MLA_KV_RANK)),
            resident((t, MLA_ROPE)),
            resident((MLA_KV_RANK, t)),
            resident(w_uvt.shape),
        ],
        out_specs=pl.BlockSpec((Q_TILE, MLA_HEADS * MLA_V), lambda i: (i, 0)),
        scratch_shapes=[pltpu.VMEM((1, rows), F32), pltpu.VMEM((1, rows), F32),
                        pltpu.VMEM((MLA_KV_RANK, rows), F32)],
        compiler_params=_cparams(("parallel",)),
        name="attn_prompt",
    )(qlat_t, qpe_t, kc, kpe, kct, w_uvt)


NEW_PAD = 16


def _attn_sample_kernel(pt_ref, q1_ref, q2_ref, ckv_hbm, kpet_hbm, cn_ref, kn_ref, wuv_ref, o_ref,
                        cbuf, pbuf, sem, *, t_new):
    n = pl.program_id(0)
    n_pages = cbuf.shape[1]
    page = cbuf.shape[2]

    def issue(seq, slot):
        def body(k, carry):
            pg = pt_ref[seq, k]
            pltpu.make_async_copy(ckv_hbm.at[pg], cbuf.at[slot, k], sem.at[0, slot]).start()
            pltpu.make_async_copy(kpet_hbm.at[pg],
                                  pbuf.at[slot, :, pl.ds(pl.multiple_of(k * page, page), page)],
                                  sem.at[1, slot]).start()
            return carry
        lax.fori_loop(0, n_pages, body, 0)

    @pl.when(n == 0)
    def _():
        issue(0, 0)

    @pl.when(n + 1 < pl.num_programs(0))
    def _():
        issue(n + 1, (n + 1) % 2)

    slot = n % 2
    pltpu.make_async_copy(ckv_hbm.at[pl.ds(0, n_pages)], cbuf.at[slot], sem.at[0, slot]).wait()
    pltpu.make_async_copy(pbuf.at[slot], pbuf.at[slot], sem.at[1, slot]).wait()

    q1 = q1_ref[0]
    q2 = q2_ref[0]
    c = cbuf[slot].reshape(n_pages * page, MLA_KV_RANK)
    s = _dot_nt(q1, c) + _dot(q2, pbuf[slot])
    cn = cn_ref[0]
    s2 = _dot_nt(q1, cn) + _dot_nt(q2, kn_ref[0])
    t_q = lax.broadcasted_iota(jnp.int32, s2.shape, 0) % t_new
    t_k = lax.broadcasted_iota(jnp.int32, s2.shape, 1)
    s2 = jnp.where(t_k <= t_q, s2, NEG_INF)
    m = jnp.maximum(jnp.max(s, axis=-1, keepdims=True), jnp.max(s2, axis=-1, keepdims=True))
    p = jnp.exp2(s - m)
    p2 = jnp.exp2(s2 - m)
    l = jnp.sum(p, axis=-1, keepdims=True) + jnp.sum(p2, axis=-1, keepdims=True)
    o_lat = ((_dot(p, c) + _dot(p2, cn)) / l).astype(BF16)
    for h in range(MLA_HEADS):
        o_ref[0, :, h * MLA_V:(h + 1) * MLA_V] = _dot(
            o_lat[h * t_new:(h + 1) * t_new], wuv_ref[h]).astype(o_ref.dtype)


def _attn_sample(page_table, q1, q2, ckv, kpet, cn, kn, w_uv, t_new):
    n, rows, _ = q1.shape
    n_pages = page_table.shape[1]
    page = ckv.shape[1]
    kern = functools.partial(_attn_sample_kernel, t_new=t_new)
    return pl.pallas_call(
        kern,
        out_shape=jax.ShapeDtypeStruct((n, t_new, MLA_HEADS * MLA_V), BF16),
        grid_spec=pltpu.PrefetchScalarGridSpec(
            num_scalar_prefetch=1,
            grid=(n,),
            in_specs=[
                pl.BlockSpec((1, rows, MLA_KV_RANK), lambda i, pt: (i, 0, 0)),
                pl.BlockSpec((1, rows, MLA_ROPE), lambda i, pt: (i, 0, 0)),
                pl.BlockSpec(memory_space=pl.ANY),
                pl.BlockSpec(memory_space=pl.ANY),
                pl.BlockSpec((1, NEW_PAD, MLA_KV_RANK), lambda i, pt: (i, 0, 0)),
                pl.BlockSpec((1, NEW_PAD, MLA_ROPE), lambda i, pt: (i, 0, 0)),
                pl.BlockSpec(w_uv.shape, lambda i, pt: (0, 0, 0)),
            ],
            out_specs=pl.BlockSpec((1, t_new, MLA_HEADS * MLA_V), lambda i, pt: (i, 0, 0)),
            scratch_shapes=[pltpu.VMEM((2, n_pages, page, MLA_KV_RANK), F32),
                            pltpu.VMEM((2, MLA_ROPE, n_pages * page), F32),
                            pltpu.SemaphoreType.DMA((2, 2))],
        ),
        compiler_params=_cparams(("arbitrary",)),
        name="attn_sample",
    )(page_table, q1, q2, ckv, kpet, cn, kn, w_uv)


def _merge_kernel(oa_ref, ob_ref, gates_ref, x_ref, wpa_ref, wpb_ref, wout_ref, gffn_ref,
                  wr_ref, br_ref, x1_ref, h2_ref, ti_ref, tg_ref):
    d = x_ref.shape[1]
    gates = gates_ref[...]
    mix = gates[:, :d] * _dot(oa_ref[...], wpa_ref[...]) + gates[:, d:] * _dot(ob_ref[...], wpb_ref[...])
    x1 = x_ref[...] + _dot(mix.astype(BF16), wout_ref[...])
    x1_ref[...] = x1
    h2 = _rms(x1, gffn_ref[...])
    h2_ref[...] = h2
    h_hi, h_mid, h_lo = _split3(h2)
    w_hi, w_mid, w_lo = wr_ref[0], wr_ref[1], wr_ref[2]
    logits = (_dot(h_hi, w_hi) + (_dot(h_hi, w_mid) + _dot(h_mid, w_hi))
              + (_dot(h_mid, w_mid) + _dot(h_hi, w_lo) + _dot(h_lo, w_hi))) + br_ref[...]
    lane = lax.broadcasted_iota(jnp.int32, logits.shape, 1)
    vals, idxs = [], []
    cur = logits
    for _ in range(TOP_K):
        mx = jnp.max(cur, axis=-1, keepdims=True)
        ix = jnp.min(jnp.where(cur == mx, lane, LANES), axis=-1, keepdims=True)
        vals.append(mx)
        idxs.append(ix)
        cur = jnp.where(lane == ix, NEG_INF, cur)
    es = [jnp.exp(v - vals[0]) for v in vals]
    tot = es[0] + es[1] + es[2] + es[3]
    ti = jnp.zeros(logits.shape, jnp.int32)
    tg = jnp.zeros(logits.shape, F32)
    for kk in range(TOP_K):
        ti = jnp.where(lane == kk, idxs[kk], ti)
        tg = jnp.where(lane == kk, es[kk] / tot, tg)
    ti_ref[...] = ti
    tg_ref[...] = tg


def _merge(o_a, o_b, gates, x_all, w_pa, w_pb, w_out, g_ffn, w_r3, b_r, tm):
    n, d = x_all.shape
    row = lambda i: (i, 0)
    full = lambda a: pl.BlockSpec(a.shape, (lambda i: (0, 0)) if a.ndim == 2 else (lambda i: (0, 0, 0)))
    return pl.pallas_call(
        _merge_kernel,
        out_shape=(jax.ShapeDtypeStruct((n, d), F32), jax.ShapeDtypeStruct((n, d), F32),
                   jax.ShapeDtypeStruct((n, LANES), jnp.int32), jax.ShapeDtypeStruct((n, LANES), F32)),
        grid=(n // tm,),
        in_specs=[pl.BlockSpec((tm, o_a.shape[1]), row), pl.BlockSpec((tm, o_b.shape[1]), row),
                  pl.BlockSpec((tm, gates.shape[1]), row), pl.BlockSpec((tm, d), row),
                  full(w_pa), full(w_pb), full(w_out), full(g_ffn), full(w_r3), full(b_r)],
        out_specs=(pl.BlockSpec((tm, d), row), pl.BlockSpec((tm, d), row),
                   pl.BlockSpec((tm, LANES), row), pl.BlockSpec((tm, LANES), row)),
        compiler_params=_cparams(("parallel",)),
        name="merge_router",
    )(o_a, o_b, gates, x_all, w_pa, w_pb, w_out, g_ffn, w_r3, b_r)


def _moe_kernel(blke_ref, nreal_ref, rowtok_ref, h2_hbm, w1_ref, b1_ref, w2_ref, b2_ref, out_ref,
                xbuf, sem, w1b, w2s, w2b):
    b = pl.program_id(0)
    n_real = nreal_ref[0]
    rows = xbuf.shape[1]
    dff = w2_ref.shape[0]
    half = LANES // 2

    def row_copy(blk, slot, r):
        tok = rowtok_ref[blk * rows + r]
        return pltpu.make_async_copy(h2_hbm.at[pl.ds(tok, 1), :], xbuf.at[slot, pl.ds(r, 1), :],
                                     sem.at[slot])

    @pl.when(b == 0)
    def _():
        def body(r, carry):
            row_copy(0, 0, r).start()
            return carry
        lax.fori_loop(0, rows, body, 0)

    prev_e = blke_ref[jnp.maximum(b - 1, 0)]

    @pl.when(jnp.logical_and(b < n_real, jnp.logical_or(b == 0, blke_ref[b] != prev_e)))
    def _():
        w1b[...] = w1_ref[...].astype(BF16)
        for cb in range(w2s.shape[0]):
            cols = slice(cb * LANES, (cb + 1) * LANES)
            for g in range(dff // LANES):
                w2s[cb, pl.ds(g * LANES, half, stride=2), :] = w2_ref[g * LANES:g * LANES + half, cols]
                w2s[cb, pl.ds(g * LANES + 1, half, stride=2), :] = w2_ref[g * LANES + half:(g + 1) * LANES, cols]
            w2b[:, cols] = w2s[cb].astype(BF16)

    def run(prefetch):
        slot = b % 2
        pltpu.make_async_copy(h2_hbm.at[pl.ds(0, rows), :], xbuf.at[slot], sem.at[slot]).wait()
        x = xbuf[slot].astype(BF16)
        if prefetch:
            for r in range(rows):
                row_copy(b + 1, 1 - slot, r).start()
        u = _dot(x, w1b[...]) + b1_ref[...]
        even = (lax.broadcasted_iota(jnp.int32, (rows, LANES), 1) % 2) == 0
        acts = []
        for t in range(dff // LANES):
            ua = u[:, 2 * t * LANES:(2 * t + 1) * LANES]
            ub = u[:, (2 * t + 1) * LANES:(2 * t + 2) * LANES]
            glu = jnp.where(even, ua, pltpu.roll(ub, 1, 1))
            lin = jnp.where(even, pltpu.roll(ua, LANES - 1, 1), ub)
            glu = jnp.minimum(glu, SWIGLU_LIMIT)
            lin = jnp.clip(lin, -SWIGLU_LIMIT, SWIGLU_LIMIT)
            acts.append((glu * _sigmoid(SWIGLU_ALPHA * glu) * (lin + 1.0)).astype(BF16))
        act = jnp.concatenate(acts, axis=1)
        out_ref[...] = _dot(act, w2b[...]) + b2_ref[...]

    @pl.when(b + 1 < n_real)
    def _():
        run(True)

    @pl.when(b + 1 == n_real)
    def _():
        run(False)

    @pl.when(b >= n_real)
    def _():
        out_ref[...] = jnp.zeros(out_ref.shape, out_ref.dtype)


def _moe(blk_e, n_real, row_tok, h2, w1, b1, w2, b2):
    n_blocks = blk_e.shape[0]
    d = h2.shape[1]
    dff = w2.shape[1]
    wspec = lambda a: pl.BlockSpec((None,) + a.shape[1:], lambda b, be, nr, rt: (be[b], 0, 0))
    return pl.pallas_call(
        _moe_kernel,
        out_shape=jax.ShapeDtypeStruct((n_blocks * MOE_ROWS, d), F32),
        grid_spec=pltpu.PrefetchScalarGridSpec(
            num_scalar_prefetch=3,
            grid=(n_blocks,),
            in_specs=[pl.BlockSpec(memory_space=pl.ANY), wspec(w1), wspec(b1), wspec(w2), wspec(b2)],
            out_specs=pl.BlockSpec((MOE_ROWS, d), lambda b, be, nr, rt: (b, 0)),
            scratch_shapes=[pltpu.VMEM((2, MOE_ROWS, d), F32), pltpu.SemaphoreType.DMA((2,)),
                            pltpu.VMEM((d, 2 * dff), BF16), pltpu.VMEM((d // LANES, dff, LANES), F32),
                            pltpu.VMEM((dff, d), BF16)],
        ),
        compiler_params=_cparams(("arbitrary",)),
        name="moe_experts",
    )(blk_e, n_real, row_tok, h2, w1, b1, w2, b2)


def _combine_kernel(pos_ref, ys_hbm, x1_ref, tg_ref, p_ref, wple_ref, wpg_ref, gple_ref, gfin_ref,
                    y_ref, buf, sem):
    i = pl.program_id(0)
    tm = x1_ref.shape[0]

    def issue(blk, slot):
        def body(r, carry):
            for kk in range(TOP_K):
                src = pos_ref[(blk * tm + r) * TOP_K + kk]
                pltpu.make_async_copy(ys_hbm.at[pl.ds(src, 1), :], buf.at[slot, kk, pl.ds(r, 1), :],
                                      sem.at[slot]).start()
            return carry
        lax.fori_loop(0, tm, body, 0)

    @pl.when(i == 0)
    def _():
        issue(0, 0)

    @pl.when(i + 1 < pl.num_programs(0))
    def _():
        issue(i + 1, (i + 1) % 2)

    slot = i % 2
    for kk in range(TOP_K):
        pltpu.make_async_copy(ys_hbm.at[pl.ds(0, tm), :], buf.at[slot, kk], sem.at[slot]).wait()
    tg = tg_ref[...]
    x2 = x1_ref[...]
    for kk in range(TOP_K):
        x2 = x2 + tg[:, kk:kk + 1] * buf[slot, kk]
    gate = _sigmoid(_dot(_rms(x2, gple_ref[...]).astype(BF16), wpg_ref[...]))
    x3 = x2 + _dot(p_ref[...].astype(BF16), wple_ref[...]) * gate
    y_ref[...] = _rms(x3, gfin_ref[...])


def _combine(pos, ys, x1, tg, p_all, w_ple, w_pg, g_ple, g_fin, tm):
    n, d = x1.shape
    row = lambda i, ps: (i, 0)
    full = lambda a: pl.BlockSpec(a.shape, lambda i, ps: (0, 0))
    return pl.pallas_call(
        _combine_kernel,
        out_shape=jax.ShapeDtypeStruct((n, d), F32),
        grid_spec=pltpu.PrefetchScalarGridSpec(
            num_scalar_prefetch=1,
            grid=(n // tm,),
            in_specs=[pl.BlockSpec(memory_space=pl.ANY),
                      pl.BlockSpec((tm, d), row), pl.BlockSpec((tm, LANES), row),
                      pl.BlockSpec((tm, p_all.shape[1]), row),
                      full(w_ple), full(w_pg), full(g_ple), full(g_fin)],
            out_specs=pl.BlockSpec((tm, d), row),
            scratch_shapes=[pltpu.VMEM((2, TOP_K, tm, d), F32), pltpu.SemaphoreType.DMA((2,))],
        ),
        compiler_params=_cparams(("arbitrary",)),
        name="combine_ple",
    )(pos, ys, x1, tg, p_all, w_ple, w_pg, g_ple, g_fin)


def _routing(top_i, n_tok):
    a = n_tok * TOP_K
    flat_e = top_i.reshape(a)
    onehot = (flat_e[:, None] == jnp.arange(N_EXPERTS, dtype=jnp.int32)[None, :]).astype(jnp.int32)
    rank = jnp.take_along_axis(jnp.cumsum(onehot, axis=0), flat_e[:, None], axis=1)[:, 0] - 1
    counts = jnp.sum(onehot, axis=0)
    padded = (counts + MOE_ROWS - 1) // MOE_ROWS * MOE_ROWS
    pad_end = jnp.cumsum(padded)
    pad_start = pad_end - padded
    dest = (pad_start[flat_e] + rank).astype(jnp.int32)
    n_blocks = -(-a // MOE_ROWS) + N_EXPERTS
    row_tok = jnp.zeros((n_blocks * MOE_ROWS,), jnp.int32).at[dest].set(
        jnp.arange(a, dtype=jnp.int32) // TOP_K)
    starts = jnp.arange(n_blocks, dtype=jnp.int32) * MOE_ROWS
    blk_e = jnp.minimum(jnp.sum((pad_end[None, :] <= starts[:, None]).astype(jnp.int32), axis=1),
                        N_EXPERTS - 1).astype(jnp.int32)
    n_real = (pad_end[-1] // MOE_ROWS).astype(jnp.int32).reshape(1)
    return dest, row_tok, blk_e, n_real


def _pick_tile(n, prefs):
    for t in prefs:
        if n % t == 0:
            return t
    raise ValueError(f"no tile in {prefs} divides {n}")


def kernel(x_prompt, x_sample, cache_ckv, cache_kpe, state_hgrn, page_table, p_prompt, p_sample, hg_lb, g_mix, w_in, g_qnorm, w_uq, w_uk, w_uv, g_kvnorm, g_onorm, w_pa, w_pb, w_out, g_ffn, w_router, b_router, w1, b1, w2, b2, g_ple, w_ple, w_pg, g_final):
    n_p, t_p, d = x_prompt.shape
    n_s, t_s, _ = x_sample.shape
    depth = w_in.shape[0]
    assert depth == 1 and n_p == 1
    n_pages = page_table.shape[1]
    page = cache_ckv.shape[2]
    past = n_pages * page
    tok_p = n_p * t_p
    tok_s = n_s * t_s
    n_tok = tok_p + tok_s
    assert tok_p % Q_TILE == 0 and tok_s % Q_TILE == 0 and Q_TILE % t_s == 0
    qw = HG_HEADS * HG_DK
    vw = HG_HEADS * HG_DV
    row2 = lambda v: v.reshape(1, -1).astype(F32)

    lb = jnp.cumsum(jax.nn.softmax(hg_lb.astype(F32), axis=0), axis=0)[0].reshape(1, qw)
    wi = w_in[0]
    hg_cols = 2 * qw + 2 * vw
    mla_cols = MLA_Q_RANK + MLA_KV_RANK + MLA_ROPE
    w_h = wi[:, :hg_cols].astype(BF16)
    w_mla = wi[:, hg_cols:hg_cols + mla_cols].astype(BF16)
    w_g = wi[:, hg_cols + mla_cols:].astype(BF16)
    wq = w_uq[0].reshape(MLA_Q_RANK, MLA_HEADS, MLA_NOPE + MLA_ROPE)
    w_uqt = jnp.concatenate([wq[:, :, :MLA_NOPE].reshape(MLA_Q_RANK, -1),
                             wq[:, :, MLA_NOPE:].reshape(MLA_Q_RANK, -1)], axis=1).T.astype(BF16)
    w_ukh = jnp.transpose(w_uk[0], (1, 0, 2)).astype(BF16)
    w_uvh = jnp.transpose(w_uv[0], (1, 0, 2)).astype(BF16)
    w_uvt = jnp.transpose(w_uv[0], (1, 2, 0)).astype(BF16)
    w_r = jnp.zeros((d, LANES), F32).at[:, :N_EXPERTS].set(w_router[0].astype(F32))
    w_r3 = jnp.stack(_split3(w_r))
    b_r = jnp.full((1, LANES), -1e30, F32).at[0, :N_EXPERTS].set(b_router[0].astype(F32))

    half = MLA_ROPE // 2
    inv = ROPE_THETA ** (-jnp.arange(half, dtype=F32) / half)
    pos = jnp.concatenate([jnp.tile(jnp.arange(t_p), n_p), jnp.tile(past + jnp.arange(t_s), n_s)])
    ang = pos.astype(F32)[:, None] * inv[None, :]
    cos, sin = jnp.cos(ang), jnp.sin(ang)
    cos2 = jnp.concatenate([cos, cos], axis=1)
    sin2 = jnp.concatenate([-sin, sin], axis=1)

    x_all = jnp.concatenate([x_prompt.reshape(tok_p, d), x_sample.reshape(tok_s, d)], axis=0)
    p_all = jnp.concatenate([p_prompt[0].reshape(tok_p, -1), p_sample[0].reshape(tok_s, -1)], axis=0)

    tm = _pick_tile(n_tok, (512, 256, 128))
    zh, gates, c_new, kpe_new, qlat_t, qpe_t, kcb, kpeb = _inproj(
        x_all, cos2, sin2, cos.T, sin.T, row2(g_mix[0]), w_h, w_mla, w_g, row2(g_qnorm[0]), w_uqt, w_ukh,
        row2(g_kvnorm[0]), tm)

    gon = row2(g_onorm[0])
    chunk_p = 64 if t_p % 64 == 0 else t_p
    oa_p, st_p = _hgrn(zh[:tok_p].reshape(n_p, t_p, -1), lb, gon,
                       jnp.zeros((n_p, HG_HEADS, HG_DK, HG_DV), F32), chunk_p, chunk_p)
    t_pad = -(-t_s // SUBLANES) * SUBLANES
    zh_s = jnp.pad(zh[tok_p:].reshape(n_s, t_s, -1), ((0, 0), (0, t_pad - t_s), (0, 0)))
    oa_s, st_s = _hgrn(zh_s, lb, gon, state_hgrn[0].astype(F32), t_pad, t_s)
    o_a = jnp.concatenate([oa_p.reshape(tok_p, vw), oa_s[:, :t_s].reshape(tok_s, vw)], axis=0)

    nq_p = tok_p // Q_TILE
    tk = _pick_tile(tok_p, (512, 256, 128))
    ob_p = _attn_prompt(qlat_t[:nq_p], qpe_t[:nq_p], kcb[:tok_p], kpeb[:tok_p], kcb[:tok_p].T, w_uvt, tk)

    def sample_rows(a):
        w = a.shape[1]
        a = a.reshape(-1, w, MLA_HEADS, Q_TILE // t_s, t_s)
        return jnp.transpose(a, (0, 3, 2, 4, 1)).reshape(n_s, MLA_HEADS * t_s, w).astype(F32)

    def new_keys(a):
        return jnp.pad(a.reshape(n_s, t_s, -1), ((0, 0), (0, NEW_PAD - t_s), (0, 0)))

    ob_s = _attn_sample(page_table, sample_rows(qlat_t[nq_p:]), sample_rows(qpe_t[nq_p:]),
                        cache_ckv[0], jnp.swapaxes(cache_kpe[0], 1, 2),
                        new_keys(c_new[tok_p:]), new_keys(kpe_new[tok_p:]), w_uvh, t_s)
    o_b = jnp.concatenate([ob_p, ob_s.reshape(tok_s, -1)], axis=0)

    x1, h2, top_i, top_g = _merge(o_a, o_b, gates, x_all, w_pa[0].astype(BF16), w_pb[0].astype(BF16),
                                  w_out[0].astype(BF16), row2(g_ffn[0]), w_r3, b_r, tm)

    dest, row_tok, blk_e, n_real = _routing(top_i[:, :TOP_K], n_tok)
    ys = _moe(blk_e, n_real, row_tok, h2, w1[0].astype(F32), b1[0][:, None, :].astype(F32),
              w2[0].astype(F32), b2[0][:, None, :].astype(F32))

    tc = _pick_tile(n_tok, (256, 128))
    y_all = _combine(dest, ys, x1, top_g, p_all, w_ple[0].astype(BF16), w_pg[0].astype(BF16),
                     row2(g_ple[0]), row2(g_final), tc)

    y_prompt = y_all[:tok_p].reshape(n_p, t_p, d)
    y_sample = y_all[tok_p:].reshape(n_s, t_s, d)
    return (y_prompt, y_sample,
            c_new[:tok_p].reshape(1, n_p, t_p, -1), kpe_new[:tok_p].reshape(1, n_p, t_p, -1), st_p[None],
            c_new[tok_p:].reshape(1, n_s, t_s, -1), kpe_new[tok_p:].reshape(1, n_s, t_s, -1), st_s[None])
```

```python
import functools
import math

import numpy as np
import jax
import jax.numpy as jnp
from jax import lax
from jax.experimental import pallas as pl
from jax.experimental.pallas import tpu as pltpu

F32 = jnp.float32
BF16 = jnp.bfloat16

HG_HEADS = 4
HG_DK = 128
HG_DV = 128
MLA_HEADS = 4
MLA_NOPE = 128
MLA_ROPE = 64
MLA_V = 128
MLA_Q_RANK = 384
MLA_KV_RANK = 256
MLA_SCALE = (MLA_NOPE + MLA_ROPE) ** -0.5
Q_SCALE = MLA_SCALE * math.log2(math.e)
ROPE_THETA = 10000.0
N_EXPERTS = 32
TOP_K = 4
SWIGLU_ALPHA = 1.702
SWIGLU_LIMIT = 7.0
EPS = 1e-6

LANES = 128
SUBLANES = 8
VMEM_LIMIT = 56 * 1024 * 1024

Q_TILE = 128
MOE_ROWS = 256
MOE_AHEAD = 2
MOE_SLOTS = MOE_AHEAD + 1
NEG_INF = float("-inf")


def _cparams(sem):
    return pltpu.CompilerParams(dimension_semantics=sem, vmem_limit_bytes=VMEM_LIMIT)


def _dot(a, b):
    return jnp.dot(a, b, preferred_element_type=F32)


def _dot_nt(a, b):
    return lax.dot_general(a, b, (((1,), (1,)), ((), ())), preferred_element_type=F32)


def _dot_tn(a, b):
    return lax.dot_general(a, b, (((0,), (0,)), ((), ())), preferred_element_type=F32)


def _rms(x, g):
    return x * lax.rsqrt(jnp.mean(x * x, axis=-1, keepdims=True) + EPS) * g


def _sigmoid(x):
    return 1.0 / (1.0 + jnp.exp(-x))


def _split3(x):
    hi = x.astype(BF16)
    r1 = x - hi.astype(F32)
    mid = r1.astype(BF16)
    lo = (r1 - mid.astype(F32)).astype(BF16)
    return hi, mid, lo


def _rope64(v, cos2, sin2):
    half = MLA_ROPE // 2
    partner = jnp.concatenate([v[:, half:], v[:, :half]], axis=1)
    return v * cos2 + partner * sin2


def _inproj_kernel(x_ref, cos_ref, sin_ref, cost_ref, sint_ref, gmix_ref, wh_ref, wmla_ref, wg_ref,
                   gq_ref, wuqt_ref, wuk_ref, gkv_ref,
                   zh_ref, gates_ref, c_ref, kpe_ref, qlat_ref, qpe_ref, kcb_ref, kpeb_ref):
    tm = x_ref.shape[0]
    half = MLA_ROPE // 2
    h = _rms(x_ref[...], gmix_ref[...]).astype(BF16)
    zh_ref[...] = _dot(h, wh_ref[...])
    gates_ref[...] = _sigmoid(_dot(h, wg_ref[...]))
    zm = _dot(h, wmla_ref[...])
    c = _rms(zm[:, MLA_Q_RANK:MLA_Q_RANK + MLA_KV_RANK], gkv_ref[...])
    c_ref[...] = c
    kcb_ref[...] = c.astype(BF16)
    kpe = _rope64(zm[:, MLA_Q_RANK + MLA_KV_RANK:], cos_ref[...], sin_ref[...])
    kpe_ref[...] = kpe
    kpeb_ref[...] = kpe.astype(BF16)
    qn_t = _rms(zm[:, :MLA_Q_RANK], gq_ref[...]).T.astype(BF16)
    q_t = _dot(wuqt_ref[...], qn_t)
    cos_t = cost_ref[...]
    sin_t = sint_ref[...]
    for hh in range(MLA_HEADS):
        q_nope = q_t[hh * MLA_NOPE:(hh + 1) * MLA_NOPE].astype(BF16)
        qlat = (_dot(wuk_ref[hh], q_nope) * Q_SCALE).astype(BF16)
        off = MLA_HEADS * MLA_NOPE + hh * MLA_ROPE
        x1 = q_t[off:off + half]
        x2 = q_t[off + half:off + MLA_ROPE]
        qpe = (jnp.concatenate([x1 * cos_t - x2 * sin_t, x1 * sin_t + x2 * cos_t], axis=0)
               * Q_SCALE).astype(BF16)
        for tb in range(tm // Q_TILE):
            qlat_ref[tb, :, hh * Q_TILE:(hh + 1) * Q_TILE] = qlat[:, tb * Q_TILE:(tb + 1) * Q_TILE]
            qpe_ref[tb, :, hh * Q_TILE:(hh + 1) * Q_TILE] = qpe[:, tb * Q_TILE:(tb + 1) * Q_TILE]


def _inproj(x_all, cos2, sin2, cos_t, sin_t, g_mix, w_h, w_mla, w_g, g_q, w_uqt, w_ukh, g_kv, tm):
    n, d = x_all.shape
    nq = n // Q_TILE
    half = MLA_ROPE // 2
    rows = MLA_HEADS * Q_TILE
    row = lambda i: (i, 0)
    col = lambda i: (0, i)
    const2 = lambda i: (0, 0)
    const3 = lambda i: (0, 0, 0)
    full = lambda a: pl.BlockSpec(a.shape, const2 if a.ndim == 2 else const3)
    out_shape = (
        jax.ShapeDtypeStruct((n, w_h.shape[1]), F32),
        jax.ShapeDtypeStruct((n, w_g.shape[1]), F32),
        jax.ShapeDtypeStruct((n, MLA_KV_RANK), F32),
        jax.ShapeDtypeStruct((n, MLA_ROPE), F32),
        jax.ShapeDtypeStruct((nq, MLA_KV_RANK, rows), BF16),
        jax.ShapeDtypeStruct((nq, MLA_ROPE, rows), BF16),
        jax.ShapeDtypeStruct((n, MLA_KV_RANK), BF16),
        jax.ShapeDtypeStruct((n, MLA_ROPE), BF16),
    )
    tq = tm // Q_TILE
    out_specs = (
        pl.BlockSpec((tm, w_h.shape[1]), row),
        pl.BlockSpec((tm, w_g.shape[1]), row),
        pl.BlockSpec((tm, MLA_KV_RANK), row),
        pl.BlockSpec((tm, MLA_ROPE), row),
        pl.BlockSpec((tq, MLA_KV_RANK, rows), lambda i: (i, 0, 0)),
        pl.BlockSpec((tq, MLA_ROPE, rows), lambda i: (i, 0, 0)),
        pl.BlockSpec((tm, MLA_KV_RANK), row),
        pl.BlockSpec((tm, MLA_ROPE), row),
    )
    in_specs = [
        pl.BlockSpec((tm, d), row),
        pl.BlockSpec((tm, MLA_ROPE), row),
        pl.BlockSpec((tm, MLA_ROPE), row),
        pl.BlockSpec((half, tm), col),
        pl.BlockSpec((half, tm), col),
        full(g_mix), full(w_h), full(w_mla), full(w_g), full(g_q), full(w_uqt), full(w_ukh), full(g_kv),
    ]
    return pl.pallas_call(
        _inproj_kernel,
        out_shape=out_shape,
        grid=(n // tm,),
        in_specs=in_specs,
        out_specs=out_specs,
        compiler_params=_cparams(("parallel",)),
        name="inproj",
    )(x_all, cos2, sin2, cos_t, sin_t, g_mix, w_h, w_mla, w_g, g_q, w_uqt, w_ukh, g_kv)


HG_DIAG = 8


def _hgrn_levels(chunk):
    levels, m = [], chunk // 2
    while m >= HG_DIAG:
        levels.append(m)
        m //= 2
    return tuple(levels)


def _hgrn_mats(chunk):
    r = np.arange(chunk)[:, None]
    j = np.arange(chunk)[None, :]
    mats = [j <= r, j > r]
    for m in _hgrn_levels(chunk):
        same = (r // (2 * m)) == (j // (2 * m))
        second = (r % (2 * m)) >= m
        mid = (r // (2 * m)) * 2 * m + m
        a = second & same & (j >= mid) & (j <= r)
        b = (~second) & same & (j > r) & (j < mid)
        mats.append(a | b)
    return np.concatenate(mats, axis=0).astype(np.float32)


def _hgrn_kernel(zh_ref, mats_ref, lb_ref, gon_ref, s0_ref, o_ref, sout_ref, st_ref, *, chunk, t_valid):
    ci = pl.program_id(1)

    @pl.when(ci == 0)
    def _():
        for h in range(HG_HEADS):
            st_ref[h] = s0_ref[0, h].T

    for r0 in range(0, zh_ref.shape[1], chunk):
        _hgrn_chunk(zh_ref, mats_ref, lb_ref, gon_ref, o_ref, st_ref, r0, chunk, t_valid)

    @pl.when(ci == pl.num_programs(1) - 1)
    def _():
        for h in range(HG_HEADS):
            sout_ref[0, h] = st_ref[h].T


def _hgrn_chunk(zh_ref, mats_ref, lb_ref, gon_ref, o_ref, st_ref, r0, chunk, t_valid):
    levels = _hgrn_levels(chunk)
    qw = HG_HEADS * HG_DK
    zh = zh_ref[0, r0:r0 + chunk]
    zq, zf, zi, zg = zh[:, :qw], zh[:, qw:2 * qw], zh[:, 2 * qw:3 * qw], zh[:, 3 * qw:]
    lb = lb_ref[...]
    f = lb + (1.0 - lb) * _sigmoid(zf)
    k = 1.0 - f
    logf = jnp.log(f)
    row = lax.broadcasted_iota(jnp.int32, (chunk, 1), 0)
    if t_valid < chunk:
        live = row < t_valid
        logf = jnp.where(live, logf, 0.0)
        k = jnp.where(live, k, 0.0)
    hi, mid, lo = _split3(logf)
    mats = mats_ref[...]
    e_all = _dot(mats, hi) + _dot(mats, mid) + _dot(mats, lo)
    b = e_all[:chunk]
    e_end = e_all[chunk:2 * chunk]
    b_end = b[chunk - 1:chunk, :]
    q = zq * (HG_DK ** -0.5)
    rr = lax.broadcasted_iota(jnp.int32, (chunk, chunk), 0)
    cc = lax.broadcasted_iota(jnp.int32, (chunk, chunk), 1)
    gon = gon_ref[...]

    for h in range(HG_HEADS):
        hs = slice(h * HG_DK, (h + 1) * HG_DK)
        qh, kh, vh, bh = q[:, hs], k[:, hs], zi[:, hs], b[:, hs]
        st = st_ref[h]
        o = _dot_nt((qh * jnp.exp(bh)).astype(BF16), st.astype(BF16))
        if levels:
            att = jnp.zeros((chunk, chunk), F32)
            for li, m in enumerate(levels):
                w = jnp.exp(e_all[(2 + li) * chunk:(3 + li) * chunk, hs])
                second = (row % (2 * m)) >= m
                qm = jnp.where(second, qh * w, 0.0).astype(BF16)
                km = jnp.where(second, 0.0, kh * w).astype(BF16)
                a = _dot_nt(qm, km)
                att = att + jnp.where((rr // (2 * m)) == (cc // (2 * m)), a, 0.0)
            o = o + _dot(att.astype(BF16), vh.astype(BF16))
        for dlt in range(min(HG_DIAG, chunk)):
            if dlt == 0:
                a = jnp.sum(qh * kh, axis=-1, keepdims=True)
                o = o + a * vh
            else:
                bs = pltpu.roll(bh, dlt, 0)
                ks = pltpu.roll(kh, dlt, 0)
                vs = pltpu.roll(vh, dlt, 0)
                w = jnp.where((row % HG_DIAG) >= dlt, jnp.exp(bh - bs), 0.0)
                a = jnp.sum(qh * ks * w, axis=-1, keepdims=True)
                o = o + a * vs
        on = _rms(o, gon)
        zgh = zg[:, hs]
        o_ref[0, r0:r0 + chunk, hs] = (on * (zgh * _sigmoid(zgh))).astype(o_ref.dtype)
        ke = (kh * jnp.exp(e_end[:, hs])).astype(BF16)
        st_ref[h] = st * jnp.exp(b_end[:, hs]) + _dot_tn(vh.astype(BF16), ke)


def _hgrn(zh, t, lb, g_onorm, s0, chunk, t_valid, step):
    n, _, w = zh.shape
    mats = jnp.asarray(_hgrn_mats(chunk), BF16)
    kern = functools.partial(_hgrn_kernel, chunk=chunk, t_valid=t_valid)
    vw = HG_HEADS * HG_DV
    return pl.pallas_call(
        kern,
        out_shape=(jax.ShapeDtypeStruct((n, t, vw), BF16),
                   jax.ShapeDtypeStruct(s0.shape, F32)),
        grid=(n, t // step),
        in_specs=[
            pl.BlockSpec((1, step, w), lambda i, c: (i, c, 0)),
            pl.BlockSpec(mats.shape, lambda i, c: (0, 0)),
            pl.BlockSpec(lb.shape, lambda i, c: (0, 0)),
            pl.BlockSpec(g_onorm.shape, lambda i, c: (0, 0)),
            pl.BlockSpec((1,) + s0.shape[1:], lambda i, c: (i, 0, 0, 0)),
        ],
        out_specs=(
            pl.BlockSpec((1, step, vw), lambda i, c: (i, c, 0)),
            pl.BlockSpec((1,) + s0.shape[1:], lambda i, c: (i, 0, 0, 0)),
        ),
        scratch_shapes=[pltpu.VMEM((HG_HEADS, HG_DV, HG_DK), F32)],
        compiler_params=_cparams(("arbitrary", "arbitrary")),
        name="hgrn2",
    )(zh, mats, lb, g_onorm, s0)


def _attn_prompt_kernel(qlat_ref, qpe_ref, kc_ref, kpe_ref, kct_ref, wuvt_ref, o_ref,
                        s0_ref, s1_ref, m_ref, l_ref, acc_ref, *, tk):
    i = pl.program_id(0)
    rows = MLA_HEADS * Q_TILE
    n_blk = (i * Q_TILE) // tk + 1
    acc_ref[...] = jnp.zeros(acc_ref.shape, F32)

    def scores(j, s_ref):
        off = pl.multiple_of(j * tk, tk)
        s_ref[...] = (_dot(kc_ref[pl.ds(off, tk), :], qlat_ref[0])
                      + _dot(kpe_ref[pl.ds(off, tk), :], qpe_ref[0]))

    def consume(j, s_ref, m_old, l_old, masked):
        off = pl.multiple_of(j * tk, tk)
        s = s_ref[...]
        if masked:
            k_pos = off + lax.broadcasted_iota(jnp.int32, s.shape, 0)
            q_pos = i * Q_TILE + lax.broadcasted_iota(jnp.int32, s.shape, 1) % Q_TILE
            s = jnp.where(k_pos <= q_pos, s, NEG_INF)
        m_new = jnp.maximum(m_old, jnp.max(s, axis=0, keepdims=True))
        p = jnp.exp2(s - m_new)
        alpha = jnp.exp2(m_old - m_new)
        l_new = alpha * l_old + jnp.sum(p, axis=0, keepdims=True)
        acc_ref[...] = alpha * acc_ref[...] + _dot(kct_ref[:, pl.ds(off, tk)], p.astype(BF16))
        return m_new, l_new

    scores(0, s0_ref)
    n_pairs = (n_blk - 1) // 2

    def body(t, carry):
        m, l = carry
        scores(2 * t + 1, s1_ref)
        m, l = consume(2 * t, s0_ref, m, l, False)
        scores(2 * t + 2, s0_ref)
        m, l = consume(2 * t + 1, s1_ref, m, l, False)
        return m, l

    m, l = lax.fori_loop(0, n_pairs, body,
                         (jnp.full((1, rows), NEG_INF, F32), jnp.zeros((1, rows), F32)))
    m_ref[...] = m
    l_ref[...] = l
    last = n_blk - 1

    @pl.when(2 * n_pairs == last)
    def _():
        _, l2 = consume(last, s0_ref, m_ref[...], l_ref[...], True)
        l_ref[...] = l2

    @pl.when(2 * n_pairs != last)
    def _():
        scores(last, s1_ref)
        m1, l1 = consume(last - 1, s0_ref, m_ref[...], l_ref[...], False)
        _, l2 = consume(last, s1_ref, m1, l1, True)
        l_ref[...] = l2

    o_lat = (acc_ref[...] / l_ref[...]).astype(BF16)
    for h in range(MLA_HEADS):
        ob_t = _dot(wuvt_ref[h], o_lat[:, h * Q_TILE:(h + 1) * Q_TILE])
        o_ref[:, h * MLA_V:(h + 1) * MLA_V] = ob_t.T.astype(o_ref.dtype)


def _attn_prompt(qlat_t, qpe_t, kc, kpe, kct, w_uvt, tk):
    nq = qlat_t.shape[0]
    t = kc.shape[0]
    rows = MLA_HEADS * Q_TILE
    kern = functools.partial(_attn_prompt_kernel, tk=tk)
    resident = lambda shape: pl.BlockSpec(shape, lambda i: (0,) * len(shape), pipeline_mode=pl.Buffered(1))
    return pl.pallas_call(
        kern,
        out_shape=jax.ShapeDtypeStruct((nq * Q_TILE, MLA_HEADS * MLA_V), BF16),
        grid=(nq,),
        in_specs=[
            pl.BlockSpec((1, MLA_KV_RANK, rows), lambda i: (i, 0, 0)),
            pl.BlockSpec((1, MLA_ROPE, rows), lambda i: (i, 0, 0)),
            resident((t, MLA_KV_RANK)),
            resident((t, MLA_ROPE)),
            resident((MLA_KV_RANK, t)),
            resident(w_uvt.shape),
        ],
        out_specs=pl.BlockSpec((Q_TILE, MLA_HEADS * MLA_V), lambda i: (i, 0)),
        scratch_shapes=[pltpu.VMEM((tk, rows), F32), pltpu.VMEM((tk, rows), F32),
                        pltpu.VMEM((1, rows), F32), pltpu.VMEM((1, rows), F32),
                        pltpu.VMEM((MLA_KV_RANK, rows), F32)],
        compiler_params=_cparams(("parallel",)),
        name="attn_prompt",
    )(qlat_t, qpe_t, kc, kpe, kct, w_uvt)


NEW_PAD = 16


def _attn_sample_kernel(pt_ref, q1_ref, q2_ref, ckv_hbm, kpet_hbm, cn_ref, kn_ref, wuv_ref, o_ref,
                        cbuf, pbuf, sem, *, t_new):
    n = pl.program_id(0)
    n_pages = cbuf.shape[1]
    page = cbuf.shape[2]

    def issue(seq, slot):
        def body(k, carry):
            pg = pt_ref[seq, k]
            pltpu.make_async_copy(ckv_hbm.at[pg], cbuf.at[slot, k], sem.at[0, slot]).start()
            pltpu.make_async_copy(kpet_hbm.at[pg],
                                  pbuf.at[slot, :, pl.ds(pl.multiple_of(k * page, page), page)],
                                  sem.at[1, slot]).start()
            return carry
        lax.fori_loop(0, n_pages, body, 0)

    @pl.when(n == 0)
    def _():
        issue(0, 0)

    @pl.when(n + 1 < pl.num_programs(0))
    def _():
        issue(n + 1, (n + 1) % 2)

    slot = n % 2
    pltpu.make_async_copy(ckv_hbm.at[pl.ds(0, n_pages)], cbuf.at[slot], sem.at[0, slot]).wait()
    pltpu.make_async_copy(pbuf.at[slot], pbuf.at[slot], sem.at[1, slot]).wait()

    q1 = q1_ref[0]
    q2 = q2_ref[0]
    c = cbuf[slot].reshape(n_pages * page, MLA_KV_RANK)
    s = _dot_nt(q1, c) + _dot(q2, pbuf[slot])
    cn = cn_ref[0]
    s2 = _dot_nt(q1, cn) + _dot_nt(q2, kn_ref[0])
    t_q = lax.broadcasted_iota(jnp.int32, s2.shape, 0) % t_new
    t_k = lax.broadcasted_iota(jnp.int32, s2.shape, 1)
    s2 = jnp.where(t_k <= t_q, s2, NEG_INF)
    m = jnp.maximum(jnp.max(s, axis=-1, keepdims=True), jnp.max(s2, axis=-1, keepdims=True))
    p = jnp.exp2(s - m)
    p2 = jnp.exp2(s2 - m)
    l = jnp.sum(p, axis=-1, keepdims=True) + jnp.sum(p2, axis=-1, keepdims=True)
    o_lat = ((_dot(p, c) + _dot(p2, cn)) / l).astype(BF16)
    for h in range(MLA_HEADS):
        o_ref[0, :, h * MLA_V:(h + 1) * MLA_V] = _dot(
            o_lat[h * t_new:(h + 1) * t_new], wuv_ref[h]).astype(o_ref.dtype)


def _attn_sample(page_table, q1, q2, ckv, kpet, cn, kn, w_uv, t_new):
    n, rows, _ = q1.shape
    n_pages = page_table.shape[1]
    page = ckv.shape[1]
    kern = functools.partial(_attn_sample_kernel, t_new=t_new)
    return pl.pallas_call(
        kern,
        out_shape=jax.ShapeDtypeStruct((n, t_new, MLA_HEADS * MLA_V), BF16),
        grid_spec=pltpu.PrefetchScalarGridSpec(
            num_scalar_prefetch=1,
            grid=(n,),
            in_specs=[
                pl.BlockSpec((1, rows, MLA_KV_RANK), lambda i, pt: (i, 0, 0)),
                pl.BlockSpec((1, rows, MLA_ROPE), lambda i, pt: (i, 0, 0)),
                pl.BlockSpec(memory_space=pl.ANY),
                pl.BlockSpec(memory_space=pl.ANY),
                pl.BlockSpec((1, NEW_PAD, MLA_KV_RANK), lambda i, pt: (i, 0, 0)),
                pl.BlockSpec((1, NEW_PAD, MLA_ROPE), lambda i, pt: (i, 0, 0)),
                pl.BlockSpec(w_uv.shape, lambda i, pt: (0, 0, 0)),
            ],
            out_specs=pl.BlockSpec((1, t_new, MLA_HEADS * MLA_V), lambda i, pt: (i, 0, 0)),
            scratch_shapes=[pltpu.VMEM((2, n_pages, page, MLA_KV_RANK), F32),
                            pltpu.VMEM((2, MLA_ROPE, n_pages * page), F32),
                            pltpu.SemaphoreType.DMA((2, 2))],
        ),
        compiler_params=_cparams(("arbitrary",)),
        name="attn_sample",
    )(page_table, q1, q2, ckv, kpet, cn, kn, w_uv)


def _merge_kernel(oa_ref, ob_ref, gates_ref, x_ref, wpa_ref, wpb_ref, wout_ref, gffn_ref,
                  wr_ref, br_ref, x1_ref, h2_ref, ti_ref, tg_ref):
    d = x_ref.shape[1]
    gates = gates_ref[...]
    mix = gates[:, :d] * _dot(oa_ref[...], wpa_ref[...]) + gates[:, d:] * _dot(ob_ref[...], wpb_ref[...])
    x1 = x_ref[...] + _dot(mix.astype(BF16), wout_ref[...])
    x1_ref[...] = x1
    h2 = _rms(x1, gffn_ref[...])
    h2_ref[...] = h2
    h_hi, h_mid, h_lo = _split3(h2)
    w_hi, w_mid, w_lo = wr_ref[0], wr_ref[1], wr_ref[2]
    logits = (_dot(h_hi, w_hi) + (_dot(h_hi, w_mid) + _dot(h_mid, w_hi))
              + (_dot(h_mid, w_mid) + _dot(h_hi, w_lo) + _dot(h_lo, w_hi))) + br_ref[...]
    lane = lax.broadcasted_iota(jnp.int32, logits.shape, 1)
    vals, idxs = [], []
    cur = logits
    for _ in range(TOP_K):
        mx = jnp.max(cur, axis=-1, keepdims=True)
        ix = jnp.min(jnp.where(cur == mx, lane, LANES), axis=-1, keepdims=True)
        vals.append(mx)
        idxs.append(ix)
        cur = jnp.where(lane == ix, NEG_INF, cur)
    es = [jnp.exp(v - vals[0]) for v in vals]
    tot = es[0] + es[1] + es[2] + es[3]
    ti = jnp.zeros(logits.shape, jnp.int32)
    tg = jnp.zeros(logits.shape, F32)
    for kk in range(TOP_K):
        ti = jnp.where(lane == kk, idxs[kk], ti)
        tg = jnp.where(lane == kk, es[kk] / tot, tg)
    ti_ref[...] = ti
    tg_ref[...] = tg


def _merge(o_a, o_b, gates, x_all, w_pa, w_pb, w_out, g_ffn, w_r3, b_r, tm):
    n, d = x_all.shape
    row = lambda i: (i, 0)
    full = lambda a: pl.BlockSpec(a.shape, (lambda i: (0, 0)) if a.ndim == 2 else (lambda i: (0, 0, 0)))
    return pl.pallas_call(
        _merge_kernel,
        out_shape=(jax.ShapeDtypeStruct((n, d), F32), jax.ShapeDtypeStruct((n, d), F32),
                   jax.ShapeDtypeStruct((n, LANES), jnp.int32), jax.ShapeDtypeStruct((n, LANES), F32)),
        grid=(n // tm,),
        in_specs=[pl.BlockSpec((tm, o_a.shape[1]), row), pl.BlockSpec((tm, o_b.shape[1]), row),
                  pl.BlockSpec((tm, gates.shape[1]), row), pl.BlockSpec((tm, d), row),
                  full(w_pa), full(w_pb), full(w_out), full(g_ffn), full(w_r3), full(b_r)],
        out_specs=(pl.BlockSpec((tm, d), row), pl.BlockSpec((tm, d), row),
                   pl.BlockSpec((tm, LANES), row), pl.BlockSpec((tm, LANES), row)),
        compiler_params=_cparams(("parallel",)),
        name="merge_router",
    )(o_a, o_b, gates, x_all, w_pa, w_pb, w_out, g_ffn, w_r3, b_r)


def _moe_kernel(blke_ref, nreal_ref, rowtok_ref, h2_hbm, w1_ref, b1_ref, w2_ref, b2_ref, out_ref,
                xbuf, sem, w1b, w2s, w2b):
    b = pl.program_id(0)
    n_real = nreal_ref[0]
    rows = xbuf.shape[1]
    dff = w2_ref.shape[0]
    half = LANES // 2

    def row_copy(blk, slot, r):
        tok = rowtok_ref[blk * rows + r]
        return pltpu.make_async_copy(h2_hbm.at[pl.ds(tok, 1), :], xbuf.at[slot, pl.ds(r, 1), :],
                                     sem.at[slot])

    for first in range(MOE_AHEAD):
        @pl.when(jnp.logical_and(b == 0, first < n_real))
        def _():
            def body(r, carry):
                row_copy(first, first, r).start()
                return carry
            lax.fori_loop(0, rows, body, 0)

    prev_e = blke_ref[jnp.maximum(b - 1, 0)]

    @pl.when(jnp.logical_and(b < n_real, jnp.logical_or(b == 0, blke_ref[b] != prev_e)))
    def _():
        w1b[...] = w1_ref[...].astype(BF16)
        for cb in range(w2s.shape[0]):
            cols = slice(cb * LANES, (cb + 1) * LANES)
            for g in range(dff // LANES):
                w2s[cb, pl.ds(g * LANES, half, stride=2), :] = w2_ref[g * LANES:g * LANES + half, cols]
                w2s[cb, pl.ds(g * LANES + 1, half, stride=2), :] = w2_ref[g * LANES + half:(g + 1) * LANES, cols]
            w2b[:, cols] = w2s[cb].astype(BF16)

    def run(prefetch):
        slot = b % MOE_SLOTS
        pltpu.make_async_copy(h2_hbm.at[pl.ds(0, rows), :], xbuf.at[slot], sem.at[slot]).wait()
        x = xbuf[slot].astype(BF16)
        if prefetch:
            nxt = (b + MOE_AHEAD) % MOE_SLOTS
            for r in range(rows):
                row_copy(b + MOE_AHEAD, nxt, r).start()
        u = _dot(x, w1b[...]) + b1_ref[...]
        even = (lax.broadcasted_iota(jnp.int32, (rows, LANES), 1) % 2) == 0
        acts = []
        for t in range(dff // LANES):
            ua = u[:, 2 * t * LANES:(2 * t + 1) * LANES]
            ub = u[:, (2 * t + 1) * LANES:(2 * t + 2) * LANES]
            glu = jnp.where(even, ua, pltpu.roll(ub, 1, 1))
            lin = jnp.where(even, pltpu.roll(ua, LANES - 1, 1), ub)
            glu = jnp.minimum(glu, SWIGLU_LIMIT)
            lin = jnp.clip(lin, -SWIGLU_LIMIT, SWIGLU_LIMIT)
            acts.append((glu * _sigmoid(SWIGLU_ALPHA * glu) * (lin + 1.0)).astype(BF16))
        act = jnp.concatenate(acts, axis=1)
        out_ref[...] = _dot(act, w2b[...]) + b2_ref[...]

    @pl.when(b + MOE_AHEAD < n_real)
    def _():
        run(True)

    @pl.when(jnp.logical_and(b < n_real, b + MOE_AHEAD >= n_real))
    def _():
        run(False)

    @pl.when(b >= n_real)
    def _():
        out_ref[...] = jnp.zeros(out_ref.shape, out_ref.dtype)


def _moe(blk_e, n_real, row_tok, h2, w1, b1, w2, b2):
    n_blocks = blk_e.shape[0]
    d = h2.shape[1]
    dff = w2.shape[1]
    wspec = lambda a: pl.BlockSpec((None,) + a.shape[1:], lambda b, be, nr, rt: (be[b], 0, 0))
    return pl.pallas_call(
        _moe_kernel,
        out_shape=jax.ShapeDtypeStruct((n_blocks * MOE_ROWS, d), F32),
        grid_spec=pltpu.PrefetchScalarGridSpec(
            num_scalar_prefetch=3,
            grid=(n_blocks,),
            in_specs=[pl.BlockSpec(memory_space=pl.ANY), wspec(w1), wspec(b1), wspec(w2), wspec(b2)],
            out_specs=pl.BlockSpec((MOE_ROWS, d), lambda b, be, nr, rt: (b, 0)),
            scratch_shapes=[pltpu.VMEM((MOE_SLOTS, MOE_ROWS, d), F32), pltpu.SemaphoreType.DMA((MOE_SLOTS,)),
                            pltpu.VMEM((d, 2 * dff), BF16), pltpu.VMEM((d // LANES, dff, LANES), F32),
                            pltpu.VMEM((dff, d), BF16)],
        ),
        compiler_params=_cparams(("arbitrary",)),
        name="moe_experts",
    )(blk_e, n_real, row_tok, h2, w1, b1, w2, b2)


def _combine_kernel(pos_ref, ys_hbm, x1_ref, tg_ref, p_ref, wple_ref, wpg_ref, gple_ref, gfin_ref,
                    y_ref, buf, sem):
    i = pl.program_id(0)
    tm = x1_ref.shape[0]

    def row_copy(blk, slot, r, kk):
        src = pos_ref[(blk * tm + r) * TOP_K + kk]
        return pltpu.make_async_copy(ys_hbm.at[pl.ds(src, 1), :], buf.at[slot, kk, pl.ds(r, 1), :],
                                     sem.at[slot])

    @pl.when(i == 0)
    def _():
        def body(r, carry):
            for kk in range(TOP_K):
                row_copy(0, 0, r, kk).start()
            return carry
        lax.fori_loop(0, tm, body, 0)

    def run(prefetch):
        slot = i % 2
        for kk in range(TOP_K):
            pltpu.make_async_copy(ys_hbm.at[pl.ds(0, tm), :], buf.at[slot, kk], sem.at[slot]).wait()
        if prefetch:
            for r in range(tm):
                for kk in range(TOP_K):
                    row_copy(i + 1, 1 - slot, r, kk).start()
        tg = tg_ref[...]
        x2 = x1_ref[...]
        for kk in range(TOP_K):
            x2 = x2 + tg[:, kk:kk + 1] * buf[slot, kk]
        gate = _sigmoid(_dot(_rms(x2, gple_ref[...]).astype(BF16), wpg_ref[...]))
        x3 = x2 + _dot(p_ref[...].astype(BF16), wple_ref[...]) * gate
        y_ref[...] = _rms(x3, gfin_ref[...])

    @pl.when(i + 1 < pl.num_programs(0))
    def _():
        run(True)

    @pl.when(i + 1 == pl.num_programs(0))
    def _():
        run(False)


def _combine(pos, ys, x1, tg, p_all, w_ple, w_pg, g_ple, g_fin, tm):
    n, d = x1.shape
    row = lambda i, ps: (i, 0)
    full = lambda a: pl.BlockSpec(a.shape, lambda i, ps: (0, 0))
    return pl.pallas_call(
        _combine_kernel,
        out_shape=jax.ShapeDtypeStruct((n, d), F32),
        grid_spec=pltpu.PrefetchScalarGridSpec(
            num_scalar_prefetch=1,
            grid=(n // tm,),
            in_specs=[pl.BlockSpec(memory_space=pl.ANY),
                      pl.BlockSpec((tm, d), row), pl.BlockSpec((tm, LANES), row),
                      pl.BlockSpec((tm, p_all.shape[1]), row),
                      full(w_ple), full(w_pg), full(g_ple), full(g_fin)],
            out_specs=pl.BlockSpec((tm, d), row),
            scratch_shapes=[pltpu.VMEM((2, TOP_K, tm, d), F32), pltpu.SemaphoreType.DMA((2,))],
        ),
        compiler_params=_cparams(("arbitrary",)),
        name="combine_ple",
    )(pos, ys, x1, tg, p_all, w_ple, w_pg, g_ple, g_fin)


def _routing(top_i, n_tok):
    a = n_tok * TOP_K
    flat_e = top_i.reshape(a)
    onehot = (flat_e[:, None] == jnp.arange(N_EXPERTS, dtype=jnp.int32)[None, :]).astype(jnp.int32)
    rank = jnp.take_along_axis(jnp.cumsum(onehot, axis=0), flat_e[:, None], axis=1)[:, 0] - 1
    counts = jnp.sum(onehot, axis=0)
    padded = (counts + MOE_ROWS - 1) // MOE_ROWS * MOE_ROWS
    pad_end = jnp.cumsum(padded)
    pad_start = pad_end - padded
    dest = (pad_start[flat_e] + rank).astype(jnp.int32)
    n_blocks = -(-a // MOE_ROWS) + N_EXPERTS
    row_tok = jnp.zeros((n_blocks * MOE_ROWS,), jnp.int32).at[dest].set(
        jnp.arange(a, dtype=jnp.int32) // TOP_K)
    starts = jnp.arange(n_blocks, dtype=jnp.int32) * MOE_ROWS
    blk_e = jnp.minimum(jnp.sum((pad_end[None, :] <= starts[:, None]).astype(jnp.int32), axis=1),
                        N_EXPERTS - 1).astype(jnp.int32)
    n_real = (pad_end[-1] // MOE_ROWS).astype(jnp.int32).reshape(1)
    return dest, row_tok, blk_e, n_real


def _pick_tile(n, prefs):
    for t in prefs:
        if n % t == 0:
            return t
    raise ValueError(f"no tile in {prefs} divides {n}")


def kernel(x_prompt, x_sample, cache_ckv, cache_kpe, state_hgrn, page_table, p_prompt, p_sample, hg_lb, g_mix, w_in, g_qnorm, w_uq, w_uk, w_uv, g_kvnorm, g_onorm, w_pa, w_pb, w_out, g_ffn, w_router, b_router, w1, b1, w2, b2, g_ple, w_ple, w_pg, g_final):
    n_p, t_p, d = x_prompt.shape
    n_s, t_s, _ = x_sample.shape
    depth = w_in.shape[0]
    assert depth == 1 and n_p == 1
    n_pages = page_table.shape[1]
    page = cache_ckv.shape[2]
    past = n_pages * page
    tok_p = n_p * t_p
    tok_s = n_s * t_s
    n_tok = tok_p + tok_s
    assert tok_p % Q_TILE == 0 and tok_s % Q_TILE == 0 and Q_TILE % t_s == 0
    qw = HG_HEADS * HG_DK
    vw = HG_HEADS * HG_DV
    row2 = lambda v: v.reshape(1, -1).astype(F32)

    lb = jnp.cumsum(jax.nn.softmax(hg_lb.astype(F32), axis=0), axis=0)[0].reshape(1, qw)
    wi = w_in[0]
    hg_cols = 2 * qw + 2 * vw
    mla_cols = MLA_Q_RANK + MLA_KV_RANK + MLA_ROPE
    w_h = wi[:, :hg_cols].astype(BF16)
    w_mla = wi[:, hg_cols:hg_cols + mla_cols].astype(BF16)
    w_g = wi[:, hg_cols + mla_cols:].astype(BF16)
    wq = w_uq[0].reshape(MLA_Q_RANK, MLA_HEADS, MLA_NOPE + MLA_ROPE)
    w_uqt = jnp.concatenate([wq[:, :, :MLA_NOPE].reshape(MLA_Q_RANK, -1),
                             wq[:, :, MLA_NOPE:].reshape(MLA_Q_RANK, -1)], axis=1).T.astype(BF16)
    w_ukh = jnp.transpose(w_uk[0], (1, 0, 2)).astype(BF16)
    w_uvh = jnp.transpose(w_uv[0], (1, 0, 2)).astype(BF16)
    w_uvt = jnp.transpose(w_uv[0], (1, 2, 0)).astype(BF16)
    w_r = jnp.zeros((d, LANES), F32).at[:, :N_EXPERTS].set(w_router[0].astype(F32))
    w_r3 = jnp.stack(_split3(w_r))
    b_r = jnp.full((1, LANES), -1e30, F32).at[0, :N_EXPERTS].set(b_router[0].astype(F32))

    half = MLA_ROPE // 2
    inv = ROPE_THETA ** (-jnp.arange(half, dtype=F32) / half)
    pos = jnp.concatenate([jnp.tile(jnp.arange(t_p), n_p), jnp.tile(past + jnp.arange(t_s), n_s)])
    ang = pos.astype(F32)[:, None] * inv[None, :]
    cos, sin = jnp.cos(ang), jnp.sin(ang)
    cos2 = jnp.concatenate([cos, cos], axis=1)
    sin2 = jnp.concatenate([-sin, sin], axis=1)

    x_all = jnp.concatenate([x_prompt.reshape(tok_p, d), x_sample.reshape(tok_s, d)], axis=0)
    p_all = jnp.concatenate([p_prompt[0].reshape(tok_p, -1), p_sample[0].reshape(tok_s, -1)], axis=0)

    tm = _pick_tile(n_tok, (512, 256, 128))
    zh, gates, c_new, kpe_new, qlat_t, qpe_t, kcb, kpeb = _inproj(
        x_all, cos2, sin2, cos.T, sin.T, row2(g_mix[0]), w_h, w_mla, w_g, row2(g_qnorm[0]), w_uqt, w_ukh,
        row2(g_kvnorm[0]), tm)

    gon = row2(g_onorm[0])
    chunk_p = 64 if t_p % 64 == 0 else t_p
    step_p = _pick_tile(t_p, (4 * chunk_p, 2 * chunk_p, chunk_p))
    oa_p, st_p = _hgrn(zh[None], t_p, lb, gon,
                       jnp.zeros((n_p, HG_HEADS, HG_DK, HG_DV), F32), chunk_p, chunk_p, step_p)
    t_pad = -(-t_s // SUBLANES) * SUBLANES
    zh_s = jnp.pad(zh[tok_p:].reshape(n_s, t_s, -1), ((0, 0), (0, t_pad - t_s), (0, 0)))
    oa_s, st_s = _hgrn(zh_s, t_pad, lb, gon, state_hgrn[0].astype(F32), t_pad, t_s, t_pad)
    o_a = jnp.concatenate([oa_p.reshape(tok_p, vw), oa_s[:, :t_s].reshape(tok_s, vw)], axis=0)

    nq_p = tok_p // Q_TILE
    tk = _pick_tile(tok_p, (512, 256, 128))
    ob_p = _attn_prompt(qlat_t[:nq_p], qpe_t[:nq_p], kcb[:tok_p], kpeb[:tok_p], kcb[:tok_p].T, w_uvt, tk)

    def sample_rows(a):
        w = a.shape[1]
        a = a.reshape(-1, w, MLA_HEADS, Q_TILE // t_s, t_s)
        return jnp.transpose(a, (0, 3, 2, 4, 1)).reshape(n_s, MLA_HEADS * t_s, w).astype(F32)

    def new_keys(a):
        return jnp.pad(a.reshape(n_s, t_s, -1), ((0, 0), (0, NEW_PAD - t_s), (0, 0)))

    ob_s = _attn_sample(page_table, sample_rows(qlat_t[nq_p:]), sample_rows(qpe_t[nq_p:]),
                        cache_ckv[0], jnp.swapaxes(cache_kpe[0], 1, 2),
                        new_keys(c_new[tok_p:]), new_keys(kpe_new[tok_p:]), w_uvh, t_s)
    o_b = jnp.concatenate([ob_p, ob_s.reshape(tok_s, -1)], axis=0)

    x1, h2, top_i, top_g = _merge(o_a, o_b, gates, x_all, w_pa[0].astype(BF16), w_pb[0].astype(BF16),
                                  w_out[0].astype(BF16), row2(g_ffn[0]), w_r3, b_r, tm)

    dest, row_tok, blk_e, n_real = _routing(top_i[:, :TOP_K], n_tok)
    ys = _moe(blk_e, n_real, row_tok, h2, w1[0].astype(F32), b1[0][:, None, :].astype(F32),
              w2[0].astype(F32), b2[0][:, None, :].astype(F32))

    tc = _pick_tile(n_tok, (256, 128))
    y_all = _combine(dest, ys, x1, top_g, p_all, w_ple[0].astype(BF16), w_pg[0].astype(BF16),
                     row2(g_ple[0]), row2(g_final), tc)

    y_prompt = y_all[:tok_p].reshape(n_p, t_p, d)
    y_sample = y_all[tok_p:].reshape(n_s, t_s, d)
    return (y_prompt, y_sample,
            c_new[:tok_p].reshape(1, n_p, t_p, -1), kpe_new[:tok_p].reshape(1, n_p, t_p, -1), st_p[None],
            c_new[tok_p:].reshape(1, n_s, t_s, -1), kpe_new[tok_p:].reshape(1, n_s, t_s, -1), st_s[None])
```

```python
import functools
import math

import numpy as np
import jax
import jax.numpy as jnp
from jax import lax
from jax.experimental import pallas as pl
from jax.experimental.pallas import tpu as pltpu

F32 = jnp.float32
BF16 = jnp.bfloat16

HG_HEADS = 4
HG_DK = 128
HG_DV = 128
MLA_HEADS = 4
MLA_NOPE = 128
MLA_ROPE = 64
MLA_V = 128
MLA_Q_RANK = 384
MLA_KV_RANK = 256
MLA_SCALE = (MLA_NOPE + MLA_ROPE) ** -0.5
Q_SCALE = MLA_SCALE * math.log2(math.e)
ROPE_THETA = 10000.0
N_EXPERTS = 32
TOP_K = 4
SWIGLU_ALPHA = 1.702
SWIGLU_LIMIT = 7.0
EPS = 1e-6

LANES = 128
SUBLANES = 8
VMEM_LIMIT = 56 * 1024 * 1024

Q_TILE = 256
MOE_ROWS = 256
NEG_INF = float("-inf")


def _cparams(sem):
    return pltpu.CompilerParams(dimension_semantics=sem, vmem_limit_bytes=VMEM_LIMIT)


def _dot(a, b):
    return jnp.dot(a, b, preferred_element_type=F32)


def _dot_nt(a, b):
    return lax.dot_general(a, b, (((1,), (1,)), ((), ())), preferred_element_type=F32)


def _dot_tn(a, b):
    return lax.dot_general(a, b, (((0,), (0,)), ((), ())), preferred_element_type=F32)


def _rms(x, g):
    return x * lax.rsqrt(jnp.mean(x * x, axis=-1, keepdims=True) + EPS) * g


def _sigmoid(x):
    return 1.0 / (1.0 + jnp.exp(-x))


def _split3(x):
    hi = x.astype(BF16)
    r1 = x - hi.astype(F32)
    mid = r1.astype(BF16)
    lo = (r1 - mid.astype(F32)).astype(BF16)
    return hi, mid, lo


def _rope64(v, cos2, sin2):
    half = MLA_ROPE // 2
    partner = jnp.concatenate([v[:, half:], v[:, :half]], axis=1)
    return v * cos2 + partner * sin2


def _inproj_kernel(x_ref, cos_ref, sin_ref, cost_ref, sint_ref, gmix_ref, wh_ref, wmla_ref, wg_ref,
                   gq_ref, wuqt_ref, wuk_ref, gkv_ref,
                   zh_ref, gates_ref, c_ref, kpe_ref, qlat_ref, qpe_ref, kcb_ref, kpeb_ref):
    tm = x_ref.shape[0]
    half = MLA_ROPE // 2
    h = _rms(x_ref[...], gmix_ref[...]).astype(BF16)
    zh_ref[...] = _dot(h, wh_ref[...])
    gates_ref[...] = _sigmoid(_dot(h, wg_ref[...]))
    zm = _dot(h, wmla_ref[...])
    c = _rms(zm[:, MLA_Q_RANK:MLA_Q_RANK + MLA_KV_RANK], gkv_ref[...])
    c_ref[...] = c
    kcb_ref[...] = c.astype(BF16)
    kpe = _rope64(zm[:, MLA_Q_RANK + MLA_KV_RANK:], cos_ref[...], sin_ref[...])
    kpe_ref[...] = kpe
    kpeb_ref[...] = kpe.astype(BF16)
    qn_t = _rms(zm[:, :MLA_Q_RANK], gq_ref[...]).T.astype(BF16)
    q_t = _dot(wuqt_ref[...], qn_t)
    cos_t = cost_ref[...]
    sin_t = sint_ref[...]
    for hh in range(MLA_HEADS):
        q_nope = q_t[hh * MLA_NOPE:(hh + 1) * MLA_NOPE].astype(BF16)
        qlat = (_dot(wuk_ref[hh], q_nope) * Q_SCALE).astype(BF16)
        off = MLA_HEADS * MLA_NOPE + hh * MLA_ROPE
        x1 = q_t[off:off + half]
        x2 = q_t[off + half:off + MLA_ROPE]
        qpe = (jnp.concatenate([x1 * cos_t - x2 * sin_t, x1 * sin_t + x2 * cos_t], axis=0)
               * Q_SCALE).astype(BF16)
        for tb in range(tm // Q_TILE):
            qlat_ref[tb, :, hh * Q_TILE:(hh + 1) * Q_TILE] = qlat[:, tb * Q_TILE:(tb + 1) * Q_TILE]
            qpe_ref[tb, :, hh * Q_TILE:(hh + 1) * Q_TILE] = qpe[:, tb * Q_TILE:(tb + 1) * Q_TILE]


def _inproj(x_all, cos2, sin2, cos_t, sin_t, g_mix, w_h, w_mla, w_g, g_q, w_uqt, w_ukh, g_kv, tm):
    n, d = x_all.shape
    nq = n // Q_TILE
    half = MLA_ROPE // 2
    rows = MLA_HEADS * Q_TILE
    row = lambda i: (i, 0)
    col = lambda i: (0, i)
    const2 = lambda i: (0, 0)
    const3 = lambda i: (0, 0, 0)
    full = lambda a: pl.BlockSpec(a.shape, const2 if a.ndim == 2 else const3)
    out_shape = (
        jax.ShapeDtypeStruct((n, w_h.shape[1]), F32),
        jax.ShapeDtypeStruct((n, w_g.shape[1]), F32),
        jax.ShapeDtypeStruct((n, MLA_KV_RANK), F32),
        jax.ShapeDtypeStruct((n, MLA_ROPE), F32),
        jax.ShapeDtypeStruct((nq, MLA_KV_RANK, rows), BF16),
        jax.ShapeDtypeStruct((nq, MLA_ROPE, rows), BF16),
        jax.ShapeDtypeStruct((n, MLA_KV_RANK), BF16),
        jax.ShapeDtypeStruct((n, MLA_ROPE), BF16),
    )
    tq = tm // Q_TILE
    out_specs = (
        pl.BlockSpec((tm, w_h.shape[1]), row),
        pl.BlockSpec((tm, w_g.shape[1]), row),
        pl.BlockSpec((tm, MLA_KV_RANK), row),
        pl.BlockSpec((tm, MLA_ROPE), row),
        pl.BlockSpec((tq, MLA_KV_RANK, rows), lambda i: (i, 0, 0)),
        pl.BlockSpec((tq, MLA_ROPE, rows), lambda i: (i, 0, 0)),
        pl.BlockSpec((tm, MLA_KV_RANK), row),
        pl.BlockSpec((tm, MLA_ROPE), row),
    )
    in_specs = [
        pl.BlockSpec((tm, d), row),
        pl.BlockSpec((tm, MLA_ROPE), row),
        pl.BlockSpec((tm, MLA_ROPE), row),
        pl.BlockSpec((half, tm), col),
        pl.BlockSpec((half, tm), col),
        full(g_mix), full(w_h), full(w_mla), full(w_g), full(g_q), full(w_uqt), full(w_ukh), full(g_kv),
    ]
    return pl.pallas_call(
        _inproj_kernel,
        out_shape=out_shape,
        grid=(n // tm,),
        in_specs=in_specs,
        out_specs=out_specs,
        compiler_params=_cparams(("parallel",)),
        name="inproj",
    )(x_all, cos2, sin2, cos_t, sin_t, g_mix, w_h, w_mla, w_g, g_q, w_uqt, w_ukh, g_kv)


HG_DIAG = 8


def _hgrn_levels(chunk):
    levels, m = [], chunk // 2
    while m >= HG_DIAG:
        levels.append(m)
        m //= 2
    return tuple(levels)


def _hgrn_mats(chunk):
    r = np.arange(chunk)[:, None]
    j = np.arange(chunk)[None, :]
    mats = [j <= r, j > r]
    for m in _hgrn_levels(chunk):
        same = (r // (2 * m)) == (j // (2 * m))
        second = (r % (2 * m)) >= m
        mid = (r // (2 * m)) * 2 * m + m
        a = second & same & (j >= mid) & (j <= r)
        b = (~second) & same & (j > r) & (j < mid)
        mats.append(a | b)
    return np.concatenate(mats, axis=0).astype(np.float32)


def _hgrn_kernel(zh_ref, mats_ref, lb_ref, gon_ref, s0_ref, o_ref, sout_ref, st_ref, *, chunk, t_valid):
    ci = pl.program_id(1)

    @pl.when(ci == 0)
    def _():
        for h in range(HG_HEADS):
            st_ref[h] = s0_ref[0, h].T

    for r0 in range(0, zh_ref.shape[1], chunk):
        _hgrn_chunk(zh_ref, mats_ref, lb_ref, gon_ref, o_ref, st_ref, r0, chunk, t_valid)

    @pl.when(ci == pl.num_programs(1) - 1)
    def _():
        for h in range(HG_HEADS):
            sout_ref[0, h] = st_ref[h].T


def _hgrn_chunk(zh_ref, mats_ref, lb_ref, gon_ref, o_ref, st_ref, r0, chunk, t_valid):
    levels = _hgrn_levels(chunk)
    qw = HG_HEADS * HG_DK
    zh = zh_ref[0, r0:r0 + chunk]
    zq, zf, zi, zg = zh[:, :qw], zh[:, qw:2 * qw], zh[:, 2 * qw:3 * qw], zh[:, 3 * qw:]
    lb = lb_ref[...]
    f = lb + (1.0 - lb) * _sigmoid(zf)
    k = 1.0 - f
    logf = jnp.log(f)
    row = lax.broadcasted_iota(jnp.int32, (chunk, 1), 0)
    if t_valid < chunk:
        live = row < t_valid
        logf = jnp.where(live, logf, 0.0)
        k = jnp.where(live, k, 0.0)
    hi, mid, lo = _split3(logf)
    mats = mats_ref[...]
    e_all = _dot(mats, hi) + _dot(mats, mid) + _dot(mats, lo)
    b = e_all[:chunk]
    e_end = e_all[chunk:2 * chunk]
    b_end = b[chunk - 1:chunk, :]
    q = zq * (HG_DK ** -0.5)
    rr = lax.broadcasted_iota(jnp.int32, (chunk, chunk), 0)
    cc = lax.broadcasted_iota(jnp.int32, (chunk, chunk), 1)
    gon = gon_ref[...]

    for h in range(HG_HEADS):
        hs = slice(h * HG_DK, (h + 1) * HG_DK)
        qh, kh, vh, bh = q[:, hs], k[:, hs], zi[:, hs], b[:, hs]
        st = st_ref[h]
        o = _dot_nt((qh * jnp.exp(bh)).astype(BF16), st.astype(BF16))
        if levels:
            att = jnp.zeros((chunk, chunk), F32)
            for li, m in enumerate(levels):
                w = jnp.exp(e_all[(2 + li) * chunk:(3 + li) * chunk, hs])
                second = (row % (2 * m)) >= m
                qm = jnp.where(second, qh * w, 0.0).astype(BF16)
                km = jnp.where(second, 0.0, kh * w).astype(BF16)
                a = _dot_nt(qm, km)
                att = att + jnp.where((rr // (2 * m)) == (cc // (2 * m)), a, 0.0)
            o = o + _dot(att.astype(BF16), vh.astype(BF16))
        for dlt in range(min(HG_DIAG, chunk)):
            if dlt == 0:
                a = jnp.sum(qh * kh, axis=-1, keepdims=True)
                o = o + a * vh
            else:
                bs = pltpu.roll(bh, dlt, 0)
                ks = pltpu.roll(kh, dlt, 0)
                vs = pltpu.roll(vh, dlt, 0)
                w = jnp.where((row % HG_DIAG) >= dlt, jnp.exp(bh - bs), 0.0)
                a = jnp.sum(qh * ks * w, axis=-1, keepdims=True)
                o = o + a * vs
        on = _rms(o, gon)
        zgh = zg[:, hs]
        o_ref[0, r0:r0 + chunk, hs] = (on * (zgh * _sigmoid(zgh))).astype(o_ref.dtype)
        ke = (kh * jnp.exp(e_end[:, hs])).astype(BF16)
        st_ref[h] = st * jnp.exp(b_end[:, hs]) + _dot_tn(vh.astype(BF16), ke)


def _hgrn(zh, t, lb, g_onorm, s0, chunk, t_valid, step):
    n, _, w = zh.shape
    mats = jnp.asarray(_hgrn_mats(chunk), BF16)
    kern = functools.partial(_hgrn_kernel, chunk=chunk, t_valid=t_valid)
    vw = HG_HEADS * HG_DV
    return pl.pallas_call(
        kern,
        out_shape=(jax.ShapeDtypeStruct((n, t, vw), BF16),
                   jax.ShapeDtypeStruct(s0.shape, F32)),
        grid=(n, t // step),
        in_specs=[
            pl.BlockSpec((1, step, w), lambda i, c: (i, c, 0)),
            pl.BlockSpec(mats.shape, lambda i, c: (0, 0)),
            pl.BlockSpec(lb.shape, lambda i, c: (0, 0)),
            pl.BlockSpec(g_onorm.shape, lambda i, c: (0, 0)),
            pl.BlockSpec((1,) + s0.shape[1:], lambda i, c: (i, 0, 0, 0)),
        ],
        out_specs=(
            pl.BlockSpec((1, step, vw), lambda i, c: (i, c, 0)),
            pl.BlockSpec((1,) + s0.shape[1:], lambda i, c: (i, 0, 0, 0)),
        ),
        scratch_shapes=[pltpu.VMEM((HG_HEADS, HG_DV, HG_DK), F32)],
        compiler_params=_cparams(("arbitrary", "arbitrary")),
        name="hgrn2",
    )(zh, mats, lb, g_onorm, s0)


def _attn_prompt_kernel(qlat_ref, qpe_ref, kc_ref, kpe_ref, kct_ref, wuvt_ref, o_ref,
                        s0_ref, s1_ref, m_ref, l_ref, acc_ref, *, tk):
    i = pl.program_id(0)
    rows = MLA_HEADS * Q_TILE
    n_blk = (i * Q_TILE) // tk + 1
    acc_ref[...] = jnp.zeros(acc_ref.shape, F32)

    def scores(j, s_ref):
        off = pl.multiple_of(j * tk, tk)
        s_ref[...] = (_dot(kc_ref[pl.ds(off, tk), :], qlat_ref[0])
                      + _dot(kpe_ref[pl.ds(off, tk), :], qpe_ref[0]))

    def consume(j, s_ref, m_old, l_old, masked):
        off = pl.multiple_of(j * tk, tk)
        s = s_ref[...]
        if masked:
            k_pos = off + lax.broadcasted_iota(jnp.int32, s.shape, 0)
            q_pos = i * Q_TILE + lax.broadcasted_iota(jnp.int32, s.shape, 1) % Q_TILE
            s = jnp.where(k_pos <= q_pos, s, NEG_INF)
        m_new = jnp.maximum(m_old, jnp.max(s, axis=0, keepdims=True))
        p = jnp.exp2(s - m_new)
        alpha = jnp.exp2(m_old - m_new)
        l_new = alpha * l_old + jnp.sum(p, axis=0, keepdims=True)
        acc_ref[...] = alpha * acc_ref[...] + _dot(kct_ref[:, pl.ds(off, tk)], p.astype(BF16))
        return m_new, l_new

    scores(0, s0_ref)
    n_pairs = (n_blk - 1) // 2

    def body(t, carry):
        m, l = carry
        scores(2 * t + 1, s1_ref)
        m, l = consume(2 * t, s0_ref, m, l, False)
        scores(2 * t + 2, s0_ref)
        m, l = consume(2 * t + 1, s1_ref, m, l, False)
        return m, l

    m, l = lax.fori_loop(0, n_pairs, body,
                         (jnp.full((1, rows), NEG_INF, F32), jnp.zeros((1, rows), F32)))
    m_ref[...] = m
    l_ref[...] = l
    last = n_blk - 1

    @pl.when(2 * n_pairs == last)
    def _():
        _, l2 = consume(last, s0_ref, m_ref[...], l_ref[...], True)
        l_ref[...] = l2

    @pl.when(2 * n_pairs != last)
    def _():
        scores(last, s1_ref)
        m1, l1 = consume(last - 1, s0_ref, m_ref[...], l_ref[...], False)
        _, l2 = consume(last, s1_ref, m1, l1, True)
        l_ref[...] = l2

    o_lat = (acc_ref[...] / l_ref[...]).astype(BF16)
    for h in range(MLA_HEADS):
        ob_t = _dot(wuvt_ref[h], o_lat[:, h * Q_TILE:(h + 1) * Q_TILE])
        o_ref[:, h * MLA_V:(h + 1) * MLA_V] = ob_t.T.astype(o_ref.dtype)


def _attn_prompt(qlat_t, qpe_t, kc, kpe, kct, w_uvt, tk):
    nq = qlat_t.shape[0]
    t = kc.shape[0]
    rows = MLA_HEADS * Q_TILE
    kern = functools.partial(_attn_prompt_kernel, tk=tk)
    resident = lambda shape: pl.BlockSpec(shape, lambda i: (0,) * len(shape), pipeline_mode=pl.Buffered(1))
    return pl.pallas_call(
        kern,
        out_shape=jax.ShapeDtypeStruct((nq * Q_TILE, MLA_HEADS * MLA_V), BF16),
        grid=(nq,),
        in_specs=[
            pl.BlockSpec((1, MLA_KV_RANK, rows), lambda i: (i, 0, 0)),
            pl.BlockSpec((1, MLA_ROPE, rows), lambda i: (i, 0, 0)),
            resident((t, MLA_KV_RANK)),
            resident((t, MLA_ROPE)),
            resident((MLA_KV_RANK, t)),
            resident(w_uvt.shape),
        ],
        out_specs=pl.BlockSpec((Q_TILE, MLA_HEADS * MLA_V), lambda i: (i, 0)),
        scratch_shapes=[pltpu.VMEM((tk, rows), F32), pltpu.VMEM((tk, rows), F32),
                        pltpu.VMEM((1, rows), F32), pltpu.VMEM((1, rows), F32),
                        pltpu.VMEM((MLA_KV_RANK, rows), F32)],
        compiler_params=_cparams(("parallel",)),
        name="attn_prompt",
    )(qlat_t, qpe_t, kc, kpe, kct, w_uvt)


NEW_PAD = 16


def _attn_sample_kernel(pt_ref, q1_ref, q2_ref, ckv_hbm, kpet_hbm, cn_ref, kn_ref, wuv_ref, o_ref,
                        cbuf, pbuf, spe, sem, *, t_new):
    n = pl.program_id(0)
    n_pages = cbuf.shape[1]
    page = cbuf.shape[2]

    def issue(seq, slot):
        def body(k, carry):
            pg = pt_ref[seq, k]
            pltpu.make_async_copy(ckv_hbm.at[pg], cbuf.at[slot, k], sem.at[0, slot]).start()
            pltpu.make_async_copy(kpet_hbm.at[pg], pbuf.at[slot, k], sem.at[1, slot]).start()
            return carry
        lax.fori_loop(0, n_pages, body, 0, unroll=4)

    @pl.when(n == 0)
    def _():
        issue(0, 0)

    @pl.when(n + 1 < pl.num_programs(0))
    def _():
        issue(n + 1, (n + 1) % 2)

    slot = n % 2
    pltpu.make_async_copy(ckv_hbm.at[pl.ds(0, n_pages)], cbuf.at[slot], sem.at[0, slot]).wait()
    pltpu.make_async_copy(kpet_hbm.at[pl.ds(0, n_pages)], pbuf.at[slot], sem.at[1, slot]).wait()

    q1 = q1_ref[0]
    q2 = q2_ref[0]
    c = cbuf[slot].reshape(n_pages * page, MLA_KV_RANK)
    for k in range(n_pages):
        spe[:, k * page:(k + 1) * page] = _dot(q2, pbuf[slot, k])
    s = _dot_nt(q1, c) + spe[...]
    cn = cn_ref[0]
    s2 = _dot_nt(q1, cn) + _dot_nt(q2, kn_ref[0])
    t_q = lax.broadcasted_iota(jnp.int32, s2.shape, 0) % t_new
    t_k = lax.broadcasted_iota(jnp.int32, s2.shape, 1)
    s2 = jnp.where(t_k <= t_q, s2, NEG_INF)
    m = jnp.maximum(jnp.max(s, axis=-1, keepdims=True), jnp.max(s2, axis=-1, keepdims=True))
    p = jnp.exp2(s - m)
    p2 = jnp.exp2(s2 - m)
    l = jnp.sum(p, axis=-1, keepdims=True) + jnp.sum(p2, axis=-1, keepdims=True)
    o_lat = ((_dot(p, c) + _dot(p2, cn)) / l).astype(BF16)
    for h in range(MLA_HEADS):
        o_ref[0, :, h * MLA_V:(h + 1) * MLA_V] = _dot(
            o_lat[h * t_new:(h + 1) * t_new], wuv_ref[h]).astype(o_ref.dtype)


def _attn_sample(page_table, q1, q2, ckv, kpet, cn, kn, w_uv, t_new):
    n, rows, _ = q1.shape
    n_pages = page_table.shape[1]
    page = ckv.shape[1]
    kern = functools.partial(_attn_sample_kernel, t_new=t_new)
    return pl.pallas_call(
        kern,
        out_shape=jax.ShapeDtypeStruct((n, t_new, MLA_HEADS * MLA_V), BF16),
        grid_spec=pltpu.PrefetchScalarGridSpec(
            num_scalar_prefetch=1,
            grid=(n,),
            in_specs=[
                pl.BlockSpec((1, rows, MLA_KV_RANK), lambda i, pt: (i, 0, 0)),
                pl.BlockSpec((1, rows, MLA_ROPE), lambda i, pt: (i, 0, 0)),
                pl.BlockSpec(memory_space=pl.ANY),
                pl.BlockSpec(memory_space=pl.ANY),
                pl.BlockSpec((1, NEW_PAD, MLA_KV_RANK), lambda i, pt: (i, 0, 0)),
                pl.BlockSpec((1, NEW_PAD, MLA_ROPE), lambda i, pt: (i, 0, 0)),
                pl.BlockSpec(w_uv.shape, lambda i, pt: (0, 0, 0)),
            ],
            out_specs=pl.BlockSpec((1, t_new, MLA_HEADS * MLA_V), lambda i, pt: (i, 0, 0)),
            scratch_shapes=[pltpu.VMEM((2, n_pages, page, MLA_KV_RANK), F32),
                            pltpu.VMEM((2, n_pages, MLA_ROPE, page), F32),
                            pltpu.VMEM((rows, n_pages * page), F32),
                            pltpu.SemaphoreType.DMA((2, 2))],
        ),
        compiler_params=_cparams(("arbitrary",)),
        name="attn_sample",
    )(page_table, q1, q2, ckv, kpet, cn, kn, w_uv)


def _merge_kernel(oa_ref, ob_ref, gates_ref, x_ref, wpa_ref, wpb_ref, wout_ref, gffn_ref,
                  wr_ref, br_ref, x1_ref, h2_ref, ti_ref, tg_ref):
    d = x_ref.shape[1]
    gates = gates_ref[...]
    mix = gates[:, :d] * _dot(oa_ref[...], wpa_ref[...]) + gates[:, d:] * _dot(ob_ref[...], wpb_ref[...])
    x1 = x_ref[...] + _dot(mix.astype(BF16), wout_ref[...])
    x1_ref[...] = x1
    h2 = _rms(x1, gffn_ref[...])
    h2_ref[...] = h2
    h_hi, h_mid, h_lo = _split3(h2)
    w_hi, w_mid, w_lo = wr_ref[0], wr_ref[1], wr_ref[2]
    logits = (_dot(h_hi, w_hi) + (_dot(h_hi, w_mid) + _dot(h_mid, w_hi))
              + (_dot(h_mid, w_mid) + _dot(h_hi, w_lo) + _dot(h_lo, w_hi))) + br_ref[...]
    lane = lax.broadcasted_iota(jnp.int32, logits.shape, 1)
    vals, idxs = [], []
    cur = logits
    for _ in range(TOP_K):
        mx = jnp.max(cur, axis=-1, keepdims=True)
        ix = jnp.min(jnp.where(cur == mx, lane, LANES), axis=-1, keepdims=True)
        vals.append(mx)
        idxs.append(ix)
        cur = jnp.where(lane == ix, NEG_INF, cur)
    es = [jnp.exp(v - vals[0]) for v in vals]
    tot = es[0] + es[1] + es[2] + es[3]
    ti = jnp.zeros(logits.shape, jnp.int32)
    tg = jnp.zeros(logits.shape, F32)
    for kk in range(TOP_K):
        ti = jnp.where(lane == kk, idxs[kk], ti)
        tg = jnp.where(lane == kk, es[kk] / tot, tg)
    ti_ref[...] = ti
    tg_ref[...] = tg


def _merge(o_a, o_b, gates, x_all, w_pa, w_pb, w_out, g_ffn, w_r3, b_r, tm):
    n, d = x_all.shape
    row = lambda i: (i, 0)
    full = lambda a: pl.BlockSpec(a.shape, (lambda i: (0, 0)) if a.ndim == 2 else (lambda i: (0, 0, 0)))
    return pl.pallas_call(
        _merge_kernel,
        out_shape=(jax.ShapeDtypeStruct((n, d), F32), jax.ShapeDtypeStruct((n, d), F32),
                   jax.ShapeDtypeStruct((n, LANES), jnp.int32), jax.ShapeDtypeStruct((n, LANES), F32)),
        grid=(n // tm,),
        in_specs=[pl.BlockSpec((tm, o_a.shape[1]), row), pl.BlockSpec((tm, o_b.shape[1]), row),
                  pl.BlockSpec((tm, gates.shape[1]), row), pl.BlockSpec((tm, d), row),
                  full(w_pa), full(w_pb), full(w_out), full(g_ffn), full(w_r3), full(b_r)],
        out_specs=(pl.BlockSpec((tm, d), row), pl.BlockSpec((tm, d), row),
                   pl.BlockSpec((tm, LANES), row), pl.BlockSpec((tm, LANES), row)),
        compiler_params=_cparams(("parallel",)),
        name="merge_router",
    )(o_a, o_b, gates, x_all, w_pa, w_pb, w_out, g_ffn, w_r3, b_r)


def _dispatch_kernel(dest_ref, cend_ref, pend_ref, nreal_ref, h2_ref, xs_ref, zblk, sem, zsem):
    i = pl.program_id(0)
    tm = h2_ref.shape[0]
    blk = zblk.shape[0]

    @pl.when(i == 0)
    def _():
        zblk[...] = jnp.zeros(zblk.shape, zblk.dtype)

        def zero_row(r):
            return pltpu.make_async_copy(zblk.at[pl.ds(0, 1), :], xs_ref.at[pl.ds(r, 1), :], zsem)

        def zero_block(j):
            return pltpu.make_async_copy(zblk, xs_ref.at[pl.ds(pl.multiple_of(j * blk, blk), blk), :], zsem)

        def run(lo, hi, copy):
            def start(r, carry):
                copy(r).start()
                return carry

            def wait(r, carry):
                copy(r).wait()
                return carry
            lax.fori_loop(lo, hi, start, 0)
            lax.fori_loop(lo, hi, wait, 0)

        for e in range(N_EXPERTS):
            run(cend_ref[e], pend_ref[e], zero_row)
        run(nreal_ref[0], xs_ref.shape[0] // blk, zero_block)

    def body(r, carry):
        for kk in range(TOP_K):
            dst = dest_ref[(i * tm + r) * TOP_K + kk]
            pltpu.make_async_copy(h2_ref.at[pl.ds(r, 1), :], xs_ref.at[pl.ds(dst, 1), :], sem).start()
        return carry
    lax.fori_loop(0, tm, body, 0, unroll=4)
    for kk in range(TOP_K):
        pltpu.make_async_copy(h2_ref, xs_ref.at[pl.ds(0, tm), :], sem).wait()


def _dispatch(dest, cnt_end, pad_end, n_real, h2, n_rows, tm):
    n, d = h2.shape
    return pl.pallas_call(
        _dispatch_kernel,
        out_shape=jax.ShapeDtypeStruct((n_rows, d), h2.dtype),
        grid_spec=pltpu.PrefetchScalarGridSpec(
            num_scalar_prefetch=4,
            grid=(n // tm,),
            in_specs=[pl.BlockSpec((tm, d), lambda i, ds, ce, pe, nr: (i, 0))],
            out_specs=pl.BlockSpec(memory_space=pl.ANY),
            scratch_shapes=[pltpu.VMEM((MOE_ROWS, d), h2.dtype), pltpu.SemaphoreType.DMA(()),
                            pltpu.SemaphoreType.DMA(())],
        ),
        compiler_params=_cparams(("arbitrary",)),
        name="moe_dispatch",
    )(dest, cnt_end, pad_end, n_real, h2)


def _moe_kernel(blke_ref, nreal_ref, xs_ref, w1_ref, b1_ref, w2_ref, b2_ref, out_ref, w1b, w2s, w2b):
    b = pl.program_id(0)
    n_real = nreal_ref[0]
    rows = xs_ref.shape[0]
    dff = w2_ref.shape[0]
    half = LANES // 2
    prev_e = blke_ref[jnp.maximum(b - 1, 0)]

    @pl.when(jnp.logical_and(b < n_real, jnp.logical_or(b == 0, blke_ref[b] != prev_e)))
    def _():
        w1b[...] = w1_ref[...].astype(BF16)
        for cb in range(w2s.shape[0]):
            cols = slice(cb * LANES, (cb + 1) * LANES)
            for g in range(dff // LANES):
                w2s[cb, pl.ds(g * LANES, half, stride=2), :] = w2_ref[g * LANES:g * LANES + half, cols]
                w2s[cb, pl.ds(g * LANES + 1, half, stride=2), :] = w2_ref[g * LANES + half:(g + 1) * LANES, cols]
            w2b[:, cols] = w2s[cb].astype(BF16)

    @pl.when(b < n_real)
    def _():
        x = xs_ref[...].astype(BF16)
        u = _dot(x, w1b[...]) + b1_ref[...]
        even = (lax.broadcasted_iota(jnp.int32, (rows, LANES), 1) % 2) == 0
        acts = []
        for t in range(dff // LANES):
            ua = u[:, 2 * t * LANES:(2 * t + 1) * LANES]
            ub = u[:, (2 * t + 1) * LANES:(2 * t + 2) * LANES]
            glu = jnp.where(even, ua, pltpu.roll(ub, 1, 1))
            lin = jnp.where(even, pltpu.roll(ua, LANES - 1, 1), ub)
            glu = jnp.minimum(glu, SWIGLU_LIMIT)
            lin = jnp.clip(lin, -SWIGLU_LIMIT, SWIGLU_LIMIT)
            acts.append((glu * _sigmoid(SWIGLU_ALPHA * glu) * (lin + 1.0)).astype(BF16))
        act = jnp.concatenate(acts, axis=1)
        out_ref[...] = _dot(act, w2b[...]) + b2_ref[...]

    @pl.when(b >= n_real)
    def _():
        out_ref[...] = jnp.zeros(out_ref.shape, out_ref.dtype)


def _moe(blk_e, n_real, xs, w1, b1, w2, b2):
    n_blocks = blk_e.shape[0]
    d = xs.shape[1]
    dff = w2.shape[1]
    wspec = lambda a: pl.BlockSpec((None,) + a.shape[1:], lambda b, be, nr: (be[b], 0, 0))
    return pl.pallas_call(
        _moe_kernel,
        out_shape=jax.ShapeDtypeStruct((n_blocks * MOE_ROWS, d), F32),
        grid_spec=pltpu.PrefetchScalarGridSpec(
            num_scalar_prefetch=2,
            grid=(n_blocks,),
            in_specs=[pl.BlockSpec((MOE_ROWS, d), lambda b, be, nr: (jnp.minimum(b, nr[0] - 1), 0)),
                      wspec(w1), wspec(b1), wspec(w2), wspec(b2)],
            out_specs=pl.BlockSpec((MOE_ROWS, d), lambda b, be, nr: (b, 0)),
            scratch_shapes=[pltpu.VMEM((d, 2 * dff), BF16), pltpu.VMEM((d // LANES, dff, LANES), F32),
                            pltpu.VMEM((dff, d), BF16)],
        ),
        compiler_params=_cparams(("arbitrary",)),
        name="moe_experts",
    )(blk_e, n_real, xs, w1, b1, w2, b2)


def _combine_kernel(pos_ref, ys_hbm, x1_ref, tg_ref, p_ref, wple_ref, wpg_ref, gple_ref, gfin_ref,
                    y_ref, buf, sem):
    i = pl.program_id(0)
    tm = x1_ref.shape[0]

    def row_copy(blk, slot, r, kk):
        src = pos_ref[(blk * tm + r) * TOP_K + kk]
        return pltpu.make_async_copy(ys_hbm.at[pl.ds(src, 1), :], buf.at[slot, kk, pl.ds(r, 1), :],
                                     sem.at[slot])

    @pl.when(i == 0)
    def _():
        def body(r, carry):
            for kk in range(TOP_K):
                row_copy(0, 0, r, kk).start()
            return carry
        lax.fori_loop(0, tm, body, 0)

    def run(prefetch):
        slot = i % 2
        for kk in range(TOP_K):
            pltpu.make_async_copy(ys_hbm.at[pl.ds(0, tm), :], buf.at[slot, kk], sem.at[slot]).wait()
        if prefetch:
            for r in range(tm):
                for kk in range(TOP_K):
                    row_copy(i + 1, 1 - slot, r, kk).start()
        tg = tg_ref[...]
        x2 = x1_ref[...]
        for kk in range(TOP_K):
            x2 = x2 + tg[:, kk:kk + 1] * buf[slot, kk]
        gate = _sigmoid(_dot(_rms(x2, gple_ref[...]).astype(BF16), wpg_ref[...]))
        x3 = x2 + _dot(p_ref[...].astype(BF16), wple_ref[...]) * gate
        y_ref[...] = _rms(x3, gfin_ref[...])

    @pl.when(i + 1 < pl.num_programs(0))
    def _():
        run(True)

    @pl.when(i + 1 == pl.num_programs(0))
    def _():
        run(False)


def _combine(pos, ys, x1, tg, p_all, w_ple, w_pg, g_ple, g_fin, tm):
    n, d = x1.shape
    row = lambda i, ps: (i, 0)
    full = lambda a: pl.BlockSpec(a.shape, lambda i, ps: (0, 0))
    return pl.pallas_call(
        _combine_kernel,
        out_shape=jax.ShapeDtypeStruct((n, d), F32),
        grid_spec=pltpu.PrefetchScalarGridSpec(
            num_scalar_prefetch=1,
            grid=(n // tm,),
            in_specs=[pl.BlockSpec(memory_space=pl.ANY),
                      pl.BlockSpec((tm, d), row), pl.BlockSpec((tm, LANES), row),
                      pl.BlockSpec((tm, p_all.shape[1]), row),
                      full(w_ple), full(w_pg), full(g_ple), full(g_fin)],
            out_specs=pl.BlockSpec((tm, d), row),
            scratch_shapes=[pltpu.VMEM((2, TOP_K, tm, d), F32), pltpu.SemaphoreType.DMA((2,))],
        ),
        compiler_params=_cparams(("arbitrary",)),
        name="combine_ple",
    )(pos, ys, x1, tg, p_all, w_ple, w_pg, g_ple, g_fin)


def _routing(top_i, n_tok):
    a = n_tok * TOP_K
    flat_e = top_i.reshape(a)
    onehot = (flat_e[:, None] == jnp.arange(N_EXPERTS, dtype=jnp.int32)[None, :]).astype(jnp.int32)
    rank = jnp.take_along_axis(jnp.cumsum(onehot, axis=0), flat_e[:, None], axis=1)[:, 0] - 1
    counts = jnp.sum(onehot, axis=0)
    padded = (counts + MOE_ROWS - 1) // MOE_ROWS * MOE_ROWS
    pad_end = jnp.cumsum(padded)
    pad_start = pad_end - padded
    dest = (pad_start[flat_e] + rank).astype(jnp.int32)
    cnt_end = (pad_start + counts).astype(jnp.int32)
    n_blocks = -(-a // MOE_ROWS) + N_EXPERTS
    starts = jnp.arange(n_blocks, dtype=jnp.int32) * MOE_ROWS
    blk_e = jnp.minimum(jnp.sum((pad_end[None, :] <= starts[:, None]).astype(jnp.int32), axis=1),
                        N_EXPERTS - 1).astype(jnp.int32)
    n_real = (pad_end[-1] // MOE_ROWS).astype(jnp.int32).reshape(1)
    return dest, cnt_end, pad_end.astype(jnp.int32), blk_e, n_real, n_blocks


def _pick_tile(n, prefs):
    for t in prefs:
        if n % t == 0:
            return t
    raise ValueError(f"no tile in {prefs} divides {n}")


def kernel(x_prompt, x_sample, cache_ckv, cache_kpe, state_hgrn, page_table, p_prompt, p_sample, hg_lb, g_mix, w_in, g_qnorm, w_uq, w_uk, w_uv, g_kvnorm, g_onorm, w_pa, w_pb, w_out, g_ffn, w_router, b_router, w1, b1, w2, b2, g_ple, w_ple, w_pg, g_final):
    n_p, t_p, d = x_prompt.shape
    n_s, t_s, _ = x_sample.shape
    depth = w_in.shape[0]
    assert depth == 1 and n_p == 1
    n_pages = page_table.shape[1]
    page = cache_ckv.shape[2]
    past = n_pages * page
    tok_p = n_p * t_p
    tok_s = n_s * t_s
    n_tok = tok_p + tok_s
    assert tok_p % Q_TILE == 0 and tok_s % Q_TILE == 0 and Q_TILE % t_s == 0
    qw = HG_HEADS * HG_DK
    vw = HG_HEADS * HG_DV
    row2 = lambda v: v.reshape(1, -1).astype(F32)

    lb = jnp.cumsum(jax.nn.softmax(hg_lb.astype(F32), axis=0), axis=0)[0].reshape(1, qw)
    wi = w_in[0]
    hg_cols = 2 * qw + 2 * vw
    mla_cols = MLA_Q_RANK + MLA_KV_RANK + MLA_ROPE
    w_h = wi[:, :hg_cols].astype(BF16)
    w_mla = wi[:, hg_cols:hg_cols + mla_cols].astype(BF16)
    w_g = wi[:, hg_cols + mla_cols:].astype(BF16)
    wq = w_uq[0].reshape(MLA_Q_RANK, MLA_HEADS, MLA_NOPE + MLA_ROPE)
    w_uqt = jnp.concatenate([wq[:, :, :MLA_NOPE].reshape(MLA_Q_RANK, -1),
                             wq[:, :, MLA_NOPE:].reshape(MLA_Q_RANK, -1)], axis=1).T.astype(BF16)
    w_ukh = jnp.transpose(w_uk[0], (1, 0, 2)).astype(BF16)
    w_uvh = jnp.transpose(w_uv[0], (1, 0, 2)).astype(BF16)
    w_uvt = jnp.transpose(w_uv[0], (1, 2, 0)).astype(BF16)
    w_r = jnp.zeros((d, LANES), F32).at[:, :N_EXPERTS].set(w_router[0].astype(F32))
    w_r3 = jnp.stack(_split3(w_r))
    b_r = jnp.full((1, LANES), -1e30, F32).at[0, :N_EXPERTS].set(b_router[0].astype(F32))

    half = MLA_ROPE // 2
    inv = ROPE_THETA ** (-jnp.arange(half, dtype=F32) / half)
    pos = jnp.concatenate([jnp.tile(jnp.arange(t_p), n_p), jnp.tile(past + jnp.arange(t_s), n_s)])
    ang = pos.astype(F32)[:, None] * inv[None, :]
    cos, sin = jnp.cos(ang), jnp.sin(ang)
    cos2 = jnp.concatenate([cos, cos], axis=1)
    sin2 = jnp.concatenate([-sin, sin], axis=1)

    x_all = jnp.concatenate([x_prompt.reshape(tok_p, d), x_sample.reshape(tok_s, d)], axis=0)
    p_all = jnp.concatenate([p_prompt[0].reshape(tok_p, -1), p_sample[0].reshape(tok_s, -1)], axis=0)

    tm = _pick_tile(n_tok, (512, 256, 128))
    zh, gates, c_new, kpe_new, qlat_t, qpe_t, kcb, kpeb = _inproj(
        x_all, cos2, sin2, cos.T, sin.T, row2(g_mix[0]), w_h, w_mla, w_g, row2(g_qnorm[0]), w_uqt, w_ukh,
        row2(g_kvnorm[0]), tm)

    gon = row2(g_onorm[0])
    chunk_p = 64 if t_p % 64 == 0 else t_p
    step_p = _pick_tile(t_p, (4 * chunk_p, 2 * chunk_p, chunk_p))
    oa_p, st_p = _hgrn(zh[None], t_p, lb, gon,
                       jnp.zeros((n_p, HG_HEADS, HG_DK, HG_DV), F32), chunk_p, chunk_p, step_p)
    t_pad = -(-t_s // SUBLANES) * SUBLANES
    zh_s = jnp.pad(zh[tok_p:].reshape(n_s, t_s, -1), ((0, 0), (0, t_pad - t_s), (0, 0)))
    oa_s, st_s = _hgrn(zh_s, t_pad, lb, gon, state_hgrn[0].astype(F32), t_pad, t_s, t_pad)
    o_a = jnp.concatenate([oa_p.reshape(tok_p, vw), oa_s[:, :t_s].reshape(tok_s, vw)], axis=0)

    nq_p = tok_p // Q_TILE
    tk = _pick_tile(tok_p, (512, 256, 128))
    ob_p = _attn_prompt(qlat_t[:nq_p], qpe_t[:nq_p], kcb[:tok_p], kpeb[:tok_p], kcb[:tok_p].T, w_uvt, tk)

    def sample_rows(a):
        w = a.shape[1]
        a = a.reshape(-1, w, MLA_HEADS, Q_TILE // t_s, t_s)
        return jnp.transpose(a, (0, 3, 2, 4, 1)).reshape(n_s, MLA_HEADS * t_s, w).astype(F32)

    def new_keys(a):
        return jnp.pad(a.reshape(n_s, t_s, -1), ((0, 0), (0, NEW_PAD - t_s), (0, 0)))

    ob_s = _attn_sample(page_table, sample_rows(qlat_t[nq_p:]), sample_rows(qpe_t[nq_p:]),
                        cache_ckv[0], jnp.swapaxes(cache_kpe[0], 1, 2),
                        new_keys(c_new[tok_p:]), new_keys(kpe_new[tok_p:]), w_uvh, t_s)
    o_b = jnp.concatenate([ob_p, ob_s.reshape(tok_s, -1)], axis=0)

    x1, h2, top_i, top_g = _merge(o_a, o_b, gates, x_all, w_pa[0].astype(BF16), w_pb[0].astype(BF16),
                                  w_out[0].astype(BF16), row2(g_ffn[0]), w_r3, b_r, tm)

    dest, cnt_end, pad_end, blk_e, n_real, n_blocks = _routing(top_i[:, :TOP_K], n_tok)
    tc = _pick_tile(n_tok, (256, 128))
    xs = _dispatch(dest, cnt_end, pad_end, n_real, h2, n_blocks * MOE_ROWS, tc)
    ys = _moe(blk_e, n_real, xs, w1[0].astype(F32), b1[0][:, None, :].astype(F32),
              w2[0].astype(F32), b2[0][:, None, :].astype(F32))

    y_all = _combine(dest, ys, x1, top_g, p_all, w_ple[0].astype(BF16), w_pg[0].astype(BF16),
                     row2(g_ple[0]), row2(g_final), tc)

    y_prompt = y_all[:tok_p].reshape(n_p, t_p, d)
    y_sample = y_all[tok_p:].reshape(n_s, t_s, d)
    return (y_prompt, y_sample,
            c_new[:tok_p].reshape(1, n_p, t_p, -1), kpe_new[:tok_p].reshape(1, n_p, t_p, -1), st_p[None],
            c_new[tok_p:].reshape(1, n_s, t_s, -1), kpe_new[tok_p:].reshape(1, n_s, t_s, -1), st_s[None])
```

```python
import functools
import math

import numpy as np
import jax
import jax.numpy as jnp
from jax import lax
from jax.experimental import pallas as pl
from jax.experimental.pallas import tpu as pltpu

F32 = jnp.float32
BF16 = jnp.bfloat16

HG_HEADS = 4
HG_DK = 128
HG_DV = 128
MLA_HEADS = 4
MLA_NOPE = 128
MLA_ROPE = 64
MLA_V = 128
MLA_Q_RANK = 384
MLA_KV_RANK = 256
MLA_SCALE = (MLA_NOPE + MLA_ROPE) ** -0.5
Q_SCALE = MLA_SCALE * math.log2(math.e)
ROPE_THETA = 10000.0
N_EXPERTS = 32
TOP_K = 4
SWIGLU_ALPHA = 1.702
SWIGLU_LIMIT = 7.0
EPS = 1e-6

LANES = 128
SUBLANES = 8
VMEM_LIMIT = 56 * 1024 * 1024

Q_TILE = 256
MOE_ROWS = 512
NEG_INF = float("-inf")


def _cparams(sem):
    return pltpu.CompilerParams(dimension_semantics=sem, vmem_limit_bytes=VMEM_LIMIT)


def _dot(a, b):
    return jnp.dot(a, b, preferred_element_type=F32)


def _dot_nt(a, b):
    return lax.dot_general(a, b, (((1,), (1,)), ((), ())), preferred_element_type=F32)


def _dot_tn(a, b):
    return lax.dot_general(a, b, (((0,), (0,)), ((), ())), preferred_element_type=F32)


def _rms(x, g):
    return x * lax.rsqrt(jnp.mean(x * x, axis=-1, keepdims=True) + EPS) * g


def _sigmoid(x):
    return 1.0 / (1.0 + jnp.exp(-x))


def _split3(x):
    hi = x.astype(BF16)
    r1 = x - hi.astype(F32)
    mid = r1.astype(BF16)
    lo = (r1 - mid.astype(F32)).astype(BF16)
    return hi, mid, lo


def _rope64(v, cos2, sin2):
    half = MLA_ROPE // 2
    partner = jnp.concatenate([v[:, half:], v[:, :half]], axis=1)
    return v * cos2 + partner * sin2


def _inproj_kernel(x_ref, cos_ref, sin_ref, cost_ref, sint_ref, gmix_ref, wh_ref, wmla_ref, wg_ref,
                   gq_ref, wuqt_ref, wuk_ref, gkv_ref,
                   zh_ref, gates_ref, c_ref, kpe_ref, qlat_ref, qpe_ref, kcb_ref, kpeb_ref):
    tm = x_ref.shape[0]
    half = MLA_ROPE // 2
    h = _rms(x_ref[...], gmix_ref[...]).astype(BF16)
    zh_ref[...] = _dot(h, wh_ref[...])
    gates_ref[...] = _sigmoid(_dot(h, wg_ref[...]))
    zm = _dot(h, wmla_ref[...])
    c = _rms(zm[:, MLA_Q_RANK:MLA_Q_RANK + MLA_KV_RANK], gkv_ref[...])
    c_ref[...] = c
    kcb_ref[...] = c.astype(BF16)
    kpe = _rope64(zm[:, MLA_Q_RANK + MLA_KV_RANK:], cos_ref[...], sin_ref[...])
    kpe_ref[...] = kpe
    kpeb_ref[...] = kpe.astype(BF16)
    qn_t = _rms(zm[:, :MLA_Q_RANK], gq_ref[...]).T.astype(BF16)
    q_t = _dot(wuqt_ref[...], qn_t)
    cos_t = cost_ref[...]
    sin_t = sint_ref[...]
    for hh in range(MLA_HEADS):
        q_nope = q_t[hh * MLA_NOPE:(hh + 1) * MLA_NOPE].astype(BF16)
        qlat = (_dot(wuk_ref[hh], q_nope) * Q_SCALE).astype(BF16)
        off = MLA_HEADS * MLA_NOPE + hh * MLA_ROPE
        x1 = q_t[off:off + half]
        x2 = q_t[off + half:off + MLA_ROPE]
        qpe = (jnp.concatenate([x1 * cos_t - x2 * sin_t, x1 * sin_t + x2 * cos_t], axis=0)
               * Q_SCALE).astype(BF16)
        for tb in range(tm // Q_TILE):
            qlat_ref[tb, :, hh * Q_TILE:(hh + 1) * Q_TILE] = qlat[:, tb * Q_TILE:(tb + 1) * Q_TILE]
            qpe_ref[tb, :, hh * Q_TILE:(hh + 1) * Q_TILE] = qpe[:, tb * Q_TILE:(tb + 1) * Q_TILE]


def _inproj(x_all, cos2, sin2, cos_t, sin_t, g_mix, w_h, w_mla, w_g, g_q, w_uqt, w_ukh, g_kv, tm):
    n, d = x_all.shape
    nq = n // Q_TILE
    half = MLA_ROPE // 2
    rows = MLA_HEADS * Q_TILE
    row = lambda i: (i, 0)
    col = lambda i: (0, i)
    const2 = lambda i: (0, 0)
    const3 = lambda i: (0, 0, 0)
    full = lambda a: pl.BlockSpec(a.shape, const2 if a.ndim == 2 else const3)
    out_shape = (
        jax.ShapeDtypeStruct((n, w_h.shape[1]), F32),
        jax.ShapeDtypeStruct((n, w_g.shape[1]), F32),
        jax.ShapeDtypeStruct((n, MLA_KV_RANK), F32),
        jax.ShapeDtypeStruct((n, MLA_ROPE), F32),
        jax.ShapeDtypeStruct((nq, MLA_KV_RANK, rows), BF16),
        jax.ShapeDtypeStruct((nq, MLA_ROPE, rows), BF16),
        jax.ShapeDtypeStruct((n, MLA_KV_RANK), BF16),
        jax.ShapeDtypeStruct((n, MLA_ROPE), BF16),
    )
    tq = tm // Q_TILE
    out_specs = (
        pl.BlockSpec((tm, w_h.shape[1]), row),
        pl.BlockSpec((tm, w_g.shape[1]), row),
        pl.BlockSpec((tm, MLA_KV_RANK), row),
        pl.BlockSpec((tm, MLA_ROPE), row),
        pl.BlockSpec((tq, MLA_KV_RANK, rows), lambda i: (i, 0, 0)),
        pl.BlockSpec((tq, MLA_ROPE, rows), lambda i: (i, 0, 0)),
        pl.BlockSpec((tm, MLA_KV_RANK), row),
        pl.BlockSpec((tm, MLA_ROPE), row),
    )
    in_specs = [
        pl.BlockSpec((tm, d), row),
        pl.BlockSpec((tm, MLA_ROPE), row),
        pl.BlockSpec((tm, MLA_ROPE), row),
        pl.BlockSpec((half, tm), col),
        pl.BlockSpec((half, tm), col),
        full(g_mix), full(w_h), full(w_mla), full(w_g), full(g_q), full(w_uqt), full(w_ukh), full(g_kv),
    ]
    return pl.pallas_call(
        _inproj_kernel,
        out_shape=out_shape,
        grid=(n // tm,),
        in_specs=in_specs,
        out_specs=out_specs,
        compiler_params=_cparams(("parallel",)),
        name="inproj",
    )(x_all, cos2, sin2, cos_t, sin_t, g_mix, w_h, w_mla, w_g, g_q, w_uqt, w_ukh, g_kv)


HG_DIAG = 8


def _hgrn_levels(chunk):
    levels, m = [], chunk // 2
    while m >= HG_DIAG:
        levels.append(m)
        m //= 2
    return tuple(levels)


def _hgrn_mats(chunk):
    r = np.arange(chunk)[:, None]
    j = np.arange(chunk)[None, :]
    mats = [j <= r, j > r]
    for m in _hgrn_levels(chunk):
        same = (r // (2 * m)) == (j // (2 * m))
        second = (r % (2 * m)) >= m
        mid = (r // (2 * m)) * 2 * m + m
        a = second & same & (j >= mid) & (j <= r)
        b = (~second) & same & (j > r) & (j < mid)
        mats.append(a | b)
    return np.concatenate(mats, axis=0).astype(np.float32)


def _hgrn_kernel(zh_ref, mats_ref, lb_ref, gon_ref, s0_ref, o_ref, sout_ref, st_ref, *, chunk, t_valid):
    ci = pl.program_id(1)

    @pl.when(ci == 0)
    def _():
        for h in range(HG_HEADS):
            st_ref[h] = s0_ref[0, h].T

    for r0 in range(0, zh_ref.shape[1], chunk):
        _hgrn_chunk(zh_ref, mats_ref, lb_ref, gon_ref, o_ref, st_ref, r0, chunk, t_valid)

    @pl.when(ci == pl.num_programs(1) - 1)
    def _():
        for h in range(HG_HEADS):
            sout_ref[0, h] = st_ref[h].T


def _hgrn_chunk(zh_ref, mats_ref, lb_ref, gon_ref, o_ref, st_ref, r0, chunk, t_valid):
    levels = _hgrn_levels(chunk)
    qw = HG_HEADS * HG_DK
    zh = zh_ref[0, r0:r0 + chunk]
    zq, zf, zi, zg = zh[:, :qw], zh[:, qw:2 * qw], zh[:, 2 * qw:3 * qw], zh[:, 3 * qw:]
    lb = lb_ref[...]
    f = lb + (1.0 - lb) * _sigmoid(zf)
    k = 1.0 - f
    logf = jnp.log(f)
    row = lax.broadcasted_iota(jnp.int32, (chunk, 1), 0)
    if t_valid < chunk:
        live = row < t_valid
        logf = jnp.where(live, logf, 0.0)
        k = jnp.where(live, k, 0.0)
    hi, mid, lo = _split3(logf)
    mats = mats_ref[...]
    e_all = _dot(mats, hi) + _dot(mats, mid) + _dot(mats, lo)
    b = e_all[:chunk]
    e_end = e_all[chunk:2 * chunk]
    b_end = b[chunk - 1:chunk, :]
    q = zq * (HG_DK ** -0.5)
    rr = lax.broadcasted_iota(jnp.int32, (chunk, chunk), 0)
    cc = lax.broadcasted_iota(jnp.int32, (chunk, chunk), 1)
    gon = gon_ref[...]

    for h in range(HG_HEADS):
        hs = slice(h * HG_DK, (h + 1) * HG_DK)
        qh, kh, vh, bh = q[:, hs], k[:, hs], zi[:, hs], b[:, hs]
        st = st_ref[h]
        o = _dot_nt((qh * jnp.exp(bh)).astype(BF16), st.astype(BF16))
        if levels:
            att = jnp.zeros((chunk, chunk), F32)
            for li, m in enumerate(levels):
                w = jnp.exp(e_all[(2 + li) * chunk:(3 + li) * chunk, hs])
                second = (row % (2 * m)) >= m
                qm = jnp.where(second, qh * w, 0.0).astype(BF16)
                km = jnp.where(second, 0.0, kh * w).astype(BF16)
                a = _dot_nt(qm, km)
                att = att + jnp.where((rr // (2 * m)) == (cc // (2 * m)), a, 0.0)
            o = o + _dot(att.astype(BF16), vh.astype(BF16))
        for dlt in range(min(HG_DIAG, chunk)):
            if dlt == 0:
                a = jnp.sum(qh * kh, axis=-1, keepdims=True)
                o = o + a * vh
            else:
                bs = pltpu.roll(bh, dlt, 0)
                ks = pltpu.roll(kh, dlt, 0)
                vs = pltpu.roll(vh, dlt, 0)
                w = jnp.where((row % HG_DIAG) >= dlt, jnp.exp(bh - bs), 0.0)
                a = jnp.sum(qh * ks * w, axis=-1, keepdims=True)
                o = o + a * vs
        on = _rms(o, gon)
        zgh = zg[:, hs]
        o_ref[0, r0:r0 + chunk, hs] = (on * (zgh * _sigmoid(zgh))).astype(o_ref.dtype)
        ke = (kh * jnp.exp(e_end[:, hs])).astype(BF16)
        st_ref[h] = st * jnp.exp(b_end[:, hs]) + _dot_tn(vh.astype(BF16), ke)


def _hgrn(zh, t, lb, g_onorm, s0, chunk, t_valid, step):
    n, _, w = zh.shape
    mats = jnp.asarray(_hgrn_mats(chunk), BF16)
    kern = functools.partial(_hgrn_kernel, chunk=chunk, t_valid=t_valid)
    vw = HG_HEADS * HG_DV
    return pl.pallas_call(
        kern,
        out_shape=(jax.ShapeDtypeStruct((n, t, vw), BF16),
                   jax.ShapeDtypeStruct(s0.shape, F32)),
        grid=(n, t // step),
        in_specs=[
            pl.BlockSpec((1, step, w), lambda i, c: (i, c, 0)),
            pl.BlockSpec(mats.shape, lambda i, c: (0, 0)),
            pl.BlockSpec(lb.shape, lambda i, c: (0, 0)),
            pl.BlockSpec(g_onorm.shape, lambda i, c: (0, 0)),
            pl.BlockSpec((1,) + s0.shape[1:], lambda i, c: (i, 0, 0, 0)),
        ],
        out_specs=(
            pl.BlockSpec((1, step, vw), lambda i, c: (i, c, 0)),
            pl.BlockSpec((1,) + s0.shape[1:], lambda i, c: (i, 0, 0, 0)),
        ),
        scratch_shapes=[pltpu.VMEM((HG_HEADS, HG_DV, HG_DK), F32)],
        compiler_params=_cparams(("arbitrary", "arbitrary")),
        name="hgrn2",
    )(zh, mats, lb, g_onorm, s0)


def _attn_prompt_kernel(qlat_ref, qpe_ref, kc_ref, kpe_ref, kct_ref, wuvt_ref, o_ref,
                        s0_ref, s1_ref, m_ref, l_ref, acc_ref, *, tk):
    i = pl.program_id(0)
    rows = MLA_HEADS * Q_TILE
    n_blk = (i * Q_TILE) // tk + 1
    acc_ref[...] = jnp.zeros(acc_ref.shape, F32)

    def scores(j, s_ref):
        off = pl.multiple_of(j * tk, tk)
        s_ref[...] = (_dot(kc_ref[pl.ds(off, tk), :], qlat_ref[0])
                      + _dot(kpe_ref[pl.ds(off, tk), :], qpe_ref[0]))

    def consume(j, s_ref, m_old, l_old, masked):
        off = pl.multiple_of(j * tk, tk)
        s = s_ref[...]
        if masked:
            k_pos = off + lax.broadcasted_iota(jnp.int32, s.shape, 0)
            q_pos = i * Q_TILE + lax.broadcasted_iota(jnp.int32, s.shape, 1) % Q_TILE
            s = jnp.where(k_pos <= q_pos, s, NEG_INF)
        m_new = jnp.maximum(m_old, jnp.max(s, axis=0, keepdims=True))
        p = jnp.exp2(s - m_new)
        alpha = jnp.exp2(m_old - m_new)
        l_new = alpha * l_old + jnp.sum(p, axis=0, keepdims=True)
        acc_ref[...] = alpha * acc_ref[...] + _dot(kct_ref[:, pl.ds(off, tk)], p.astype(BF16))
        return m_new, l_new

    scores(0, s0_ref)
    n_pairs = (n_blk - 1) // 2

    def body(t, carry):
        m, l = carry
        scores(2 * t + 1, s1_ref)
        m, l = consume(2 * t, s0_ref, m, l, False)
        scores(2 * t + 2, s0_ref)
        m, l = consume(2 * t + 1, s1_ref, m, l, False)
        return m, l

    m, l = lax.fori_loop(0, n_pairs, body,
                         (jnp.full((1, rows), NEG_INF, F32), jnp.zeros((1, rows), F32)))
    m_ref[...] = m
    l_ref[...] = l
    last = n_blk - 1

    @pl.when(2 * n_pairs == last)
    def _():
        _, l2 = consume(last, s0_ref, m_ref[...], l_ref[...], True)
        l_ref[...] = l2

    @pl.when(2 * n_pairs != last)
    def _():
        scores(last, s1_ref)
        m1, l1 = consume(last - 1, s0_ref, m_ref[...], l_ref[...], False)
        _, l2 = consume(last, s1_ref, m1, l1, True)
        l_ref[...] = l2

    o_lat = (acc_ref[...] / l_ref[...]).astype(BF16)
    for h in range(MLA_HEADS):
        ob_t = _dot(wuvt_ref[h], o_lat[:, h * Q_TILE:(h + 1) * Q_TILE])
        o_ref[:, h * MLA_V:(h + 1) * MLA_V] = ob_t.T.astype(o_ref.dtype)


def _attn_prompt(qlat_t, qpe_t, kc, kpe, kct, w_uvt, tk):
    t = kct.shape[1]
    nq = t // Q_TILE
    rows = MLA_HEADS * Q_TILE
    kern = functools.partial(_attn_prompt_kernel, tk=tk)
    resident = lambda shape: pl.BlockSpec(shape, lambda i: (0,) * len(shape), pipeline_mode=pl.Buffered(1))
    return pl.pallas_call(
        kern,
        out_shape=jax.ShapeDtypeStruct((nq * Q_TILE, MLA_HEADS * MLA_V), BF16),
        grid=(nq,),
        in_specs=[
            pl.BlockSpec((1, MLA_KV_RANK, rows), lambda i: (i, 0, 0)),
            pl.BlockSpec((1, MLA_ROPE, rows), lambda i: (i, 0, 0)),
            resident((t, MLA_KV_RANK)),
            resident((t, MLA_ROPE)),
            resident((MLA_KV_RANK, t)),
            resident(w_uvt.shape),
        ],
        out_specs=pl.BlockSpec((Q_TILE, MLA_HEADS * MLA_V), lambda i: (i, 0)),
        scratch_shapes=[pltpu.VMEM((tk, rows), F32), pltpu.VMEM((tk, rows), F32),
                        pltpu.VMEM((1, rows), F32), pltpu.VMEM((1, rows), F32),
                        pltpu.VMEM((MLA_KV_RANK, rows), F32)],
        compiler_params=_cparams(("parallel",)),
        name="attn_prompt",
    )(qlat_t, qpe_t, kc, kpe, kct, w_uvt)


NEW_PAD = 16


def _attn_sample_kernel(pt_ref, q1_ref, q2_ref, ckv_hbm, kpet_hbm, cn_ref, kn_ref, wuv_ref, o_ref,
                        cbuf, pbuf, spe, sem, *, t_new):
    n = pl.program_id(0)
    n_pages = cbuf.shape[1]
    page = cbuf.shape[2]

    def issue(seq, slot):
        def body(k, carry):
            pg = pt_ref[seq, k]
            pltpu.make_async_copy(ckv_hbm.at[pg], cbuf.at[slot, k], sem.at[0, slot]).start()
            pltpu.make_async_copy(kpet_hbm.at[pg], pbuf.at[slot, k], sem.at[1, slot]).start()
            return carry
        lax.fori_loop(0, n_pages, body, 0, unroll=4)

    @pl.when(n == 0)
    def _():
        issue(0, 0)

    @pl.when(n + 1 < pl.num_programs(0))
    def _():
        issue(n + 1, (n + 1) % 2)

    slot = n % 2
    pltpu.make_async_copy(ckv_hbm.at[pl.ds(0, n_pages)], cbuf.at[slot], sem.at[0, slot]).wait()
    pltpu.make_async_copy(kpet_hbm.at[pl.ds(0, n_pages)], pbuf.at[slot], sem.at[1, slot]).wait()

    q1 = q1_ref[0]
    q2 = q2_ref[0]
    c = cbuf[slot].reshape(n_pages * page, MLA_KV_RANK)
    for k in range(n_pages):
        spe[:, k * page:(k + 1) * page] = _dot(q2, pbuf[slot, k])
    s = _dot_nt(q1, c) + spe[...]
    cn = cn_ref[0]
    s2 = _dot_nt(q1, cn) + _dot_nt(q2, kn_ref[0])
    t_q = lax.broadcasted_iota(jnp.int32, s2.shape, 0) % t_new
    t_k = lax.broadcasted_iota(jnp.int32, s2.shape, 1)
    s2 = jnp.where(t_k <= t_q, s2, NEG_INF)
    m = jnp.maximum(jnp.max(s, axis=-1, keepdims=True), jnp.max(s2, axis=-1, keepdims=True))
    p = jnp.exp2(s - m)
    p2 = jnp.exp2(s2 - m)
    l = jnp.sum(p, axis=-1, keepdims=True) + jnp.sum(p2, axis=-1, keepdims=True)
    o_lat = ((_dot(p, c) + _dot(p2, cn)) / l).astype(BF16)
    for h in range(MLA_HEADS):
        o_ref[0, :, h * MLA_V:(h + 1) * MLA_V] = _dot(
            o_lat[h * t_new:(h + 1) * t_new], wuv_ref[h]).astype(o_ref.dtype)


def _attn_sample(page_table, q1, q2, ckv, kpet, cn, kn, w_uv, t_new):
    n, rows, _ = q1.shape
    n_pages = page_table.shape[1]
    page = ckv.shape[1]
    kern = functools.partial(_attn_sample_kernel, t_new=t_new)
    return pl.pallas_call(
        kern,
        out_shape=jax.ShapeDtypeStruct((n, t_new, MLA_HEADS * MLA_V), BF16),
        grid_spec=pltpu.PrefetchScalarGridSpec(
            num_scalar_prefetch=1,
            grid=(n,),
            in_specs=[
                pl.BlockSpec((1, rows, MLA_KV_RANK), lambda i, pt: (i, 0, 0)),
                pl.BlockSpec((1, rows, MLA_ROPE), lambda i, pt: (i, 0, 0)),
                pl.BlockSpec(memory_space=pl.ANY),
                pl.BlockSpec(memory_space=pl.ANY),
                pl.BlockSpec((1, NEW_PAD, MLA_KV_RANK), lambda i, pt: (i, 0, 0)),
                pl.BlockSpec((1, NEW_PAD, MLA_ROPE), lambda i, pt: (i, 0, 0)),
                pl.BlockSpec(w_uv.shape, lambda i, pt: (0, 0, 0)),
            ],
            out_specs=pl.BlockSpec((1, t_new, MLA_HEADS * MLA_V), lambda i, pt: (i, 0, 0)),
            scratch_shapes=[pltpu.VMEM((2, n_pages, page, MLA_KV_RANK), F32),
                            pltpu.VMEM((2, n_pages, MLA_ROPE, page), F32),
                            pltpu.VMEM((rows, n_pages * page), F32),
                            pltpu.SemaphoreType.DMA((2, 2))],
        ),
        compiler_params=_cparams(("arbitrary",)),
        name="attn_sample",
    )(page_table, q1, q2, ckv, kpet, cn, kn, w_uv)


def _merge_kernel(oa_ref, ob_ref, gates_ref, x_ref, wpa_ref, wpb_ref, wout_ref, gffn_ref,
                  wr_ref, br_ref, x1_ref, h2_ref, ti_ref, tg_ref):
    d = x_ref.shape[1]
    gates = gates_ref[...]
    mix = gates[:, :d] * _dot(oa_ref[...], wpa_ref[...]) + gates[:, d:] * _dot(ob_ref[...], wpb_ref[...])
    x1 = x_ref[...] + _dot(mix.astype(BF16), wout_ref[...])
    x1_ref[...] = x1
    h2 = _rms(x1, gffn_ref[...])
    h2_ref[...] = h2
    h_hi, h_mid, h_lo = _split3(h2)
    w_hi, w_mid, w_lo = wr_ref[0], wr_ref[1], wr_ref[2]
    logits = (_dot(h_hi, w_hi) + (_dot(h_hi, w_mid) + _dot(h_mid, w_hi))
              + (_dot(h_mid, w_mid) + _dot(h_hi, w_lo) + _dot(h_lo, w_hi))) + br_ref[...]
    lane = lax.broadcasted_iota(jnp.int32, logits.shape, 1)
    vals, idxs = [], []
    cur = logits
    for _ in range(TOP_K):
        mx = jnp.max(cur, axis=-1, keepdims=True)
        ix = jnp.min(jnp.where(cur == mx, lane, LANES), axis=-1, keepdims=True)
        vals.append(mx)
        idxs.append(ix)
        cur = jnp.where(lane == ix, NEG_INF, cur)
    es = [jnp.exp(v - vals[0]) for v in vals]
    tot = es[0] + es[1] + es[2] + es[3]
    ti = jnp.zeros(logits.shape, jnp.int32)
    tg = jnp.zeros(logits.shape, F32)
    for kk in range(TOP_K):
        ti = jnp.where(lane == kk, idxs[kk], ti)
        tg = jnp.where(lane == kk, es[kk] / tot, tg)
    ti_ref[...] = ti
    tg_ref[...] = tg


def _merge(o_a, o_b, gates, x_all, w_pa, w_pb, w_out, g_ffn, w_r3, b_r, tm):
    n, d = x_all.shape
    row = lambda i: (i, 0)
    full = lambda a: pl.BlockSpec(a.shape, (lambda i: (0, 0)) if a.ndim == 2 else (lambda i: (0, 0, 0)))
    return pl.pallas_call(
        _merge_kernel,
        out_shape=(jax.ShapeDtypeStruct((n, d), F32), jax.ShapeDtypeStruct((n, d), F32),
                   jax.ShapeDtypeStruct((n, LANES), jnp.int32), jax.ShapeDtypeStruct((n, LANES), F32)),
        grid=(n // tm,),
        in_specs=[pl.BlockSpec((tm, o_a.shape[1]), row), pl.BlockSpec((tm, o_b.shape[1]), row),
                  pl.BlockSpec((tm, gates.shape[1]), row), pl.BlockSpec((tm, d), row),
                  full(w_pa), full(w_pb), full(w_out), full(g_ffn), full(w_r3), full(b_r)],
        out_specs=(pl.BlockSpec((tm, d), row), pl.BlockSpec((tm, d), row),
                   pl.BlockSpec((tm, LANES), row), pl.BlockSpec((tm, LANES), row)),
        compiler_params=_cparams(("parallel",)),
        name="merge_router",
    )(o_a, o_b, gates, x_all, w_pa, w_pb, w_out, g_ffn, w_r3, b_r)


def _dispatch_kernel(dest_ref, cend_ref, pend_ref, nreal_ref, h2_ref, xs_ref, zblk, sem, zsem):
    i = pl.program_id(0)
    tm = h2_ref.shape[0]
    blk = zblk.shape[0]

    @pl.when(i == 0)
    def _():
        zblk[...] = jnp.zeros(zblk.shape, zblk.dtype)

        def zero_row(r):
            return pltpu.make_async_copy(zblk.at[pl.ds(0, 1), :], xs_ref.at[pl.ds(r, 1), :], zsem)

        def zero_block(j):
            return pltpu.make_async_copy(zblk, xs_ref.at[pl.ds(pl.multiple_of(j * blk, blk), blk), :], zsem)

        def run(lo, hi, copy):
            def start(r, carry):
                copy(r).start()
                return carry

            def wait(r, carry):
                copy(r).wait()
                return carry
            lax.fori_loop(lo, hi, start, 0)
            lax.fori_loop(lo, hi, wait, 0)

        for e in range(N_EXPERTS):
            run(cend_ref[e], pend_ref[e], zero_row)
        run(nreal_ref[0], xs_ref.shape[0] // blk, zero_block)

    def body(r, carry):
        for kk in range(TOP_K):
            dst = dest_ref[(i * tm + r) * TOP_K + kk]
            pltpu.make_async_copy(h2_ref.at[pl.ds(r, 1), :], xs_ref.at[pl.ds(dst, 1), :], sem).start()
        return carry
    lax.fori_loop(0, tm, body, 0, unroll=4)
    for kk in range(TOP_K):
        pltpu.make_async_copy(h2_ref, xs_ref.at[pl.ds(0, tm), :], sem).wait()


def _dispatch(dest, cnt_end, pad_end, n_real, h2, n_rows, tm):
    n, d = h2.shape
    return pl.pallas_call(
        _dispatch_kernel,
        out_shape=jax.ShapeDtypeStruct((n_rows, d), h2.dtype),
        grid_spec=pltpu.PrefetchScalarGridSpec(
            num_scalar_prefetch=4,
            grid=(n // tm,),
            in_specs=[pl.BlockSpec((tm, d), lambda i, ds, ce, pe, nr: (i, 0))],
            out_specs=pl.BlockSpec(memory_space=pl.ANY),
            scratch_shapes=[pltpu.VMEM((MOE_ROWS, d), h2.dtype), pltpu.SemaphoreType.DMA(()),
                            pltpu.SemaphoreType.DMA(())],
        ),
        compiler_params=_cparams(("arbitrary",)),
        name="moe_dispatch",
    )(dest, cnt_end, pad_end, n_real, h2)


def _moe_kernel(blke_ref, nreal_ref, xs_ref, w1_ref, b1_ref, w2_ref, b2_ref, out_ref, w1b, w2s, w2b):
    b = pl.program_id(0)
    n_real = nreal_ref[0]
    rows = xs_ref.shape[0]
    dff = w2_ref.shape[0]
    half = LANES // 2
    prev_e = blke_ref[jnp.maximum(b - 1, 0)]

    @pl.when(jnp.logical_and(b < n_real, jnp.logical_or(b == 0, blke_ref[b] != prev_e)))
    def _():
        w1b[...] = w1_ref[...].astype(BF16)
        for cb in range(w2s.shape[0]):
            cols = slice(cb * LANES, (cb + 1) * LANES)
            for g in range(dff // LANES):
                w2s[cb, pl.ds(g * LANES, half, stride=2), :] = w2_ref[g * LANES:g * LANES + half, cols]
                w2s[cb, pl.ds(g * LANES + 1, half, stride=2), :] = w2_ref[g * LANES + half:(g + 1) * LANES, cols]
            w2b[:, cols] = w2s[cb].astype(BF16)

    @pl.when(b < n_real)
    def _():
        x = xs_ref[...].astype(BF16)
        u = _dot(x, w1b[...]) + b1_ref[...]
        even = (lax.broadcasted_iota(jnp.int32, (rows, LANES), 1) % 2) == 0
        acts = []
        for t in range(dff // LANES):
            ua = u[:, 2 * t * LANES:(2 * t + 1) * LANES]
            ub = u[:, (2 * t + 1) * LANES:(2 * t + 2) * LANES]
            glu = jnp.where(even, ua, pltpu.roll(ub, 1, 1))
            lin = jnp.where(even, pltpu.roll(ua, LANES - 1, 1), ub)
            glu = jnp.minimum(glu, SWIGLU_LIMIT)
            lin = jnp.clip(lin, -SWIGLU_LIMIT, SWIGLU_LIMIT)
            acts.append((glu * _sigmoid(SWIGLU_ALPHA * glu) * (lin + 1.0)).astype(BF16))
        act = jnp.concatenate(acts, axis=1)
        out_ref[...] = _dot(act, w2b[...]) + b2_ref[...]

    @pl.when(b >= n_real)
    def _():
        out_ref[...] = jnp.zeros(out_ref.shape, out_ref.dtype)


def _moe(blk_e, n_real, xs, w1, b1, w2, b2):
    n_blocks = blk_e.shape[0]
    d = xs.shape[1]
    dff = w2.shape[1]
    wspec = lambda a: pl.BlockSpec((None,) + a.shape[1:], lambda b, be, nr: (be[b], 0, 0))
    return pl.pallas_call(
        _moe_kernel,
        out_shape=jax.ShapeDtypeStruct((n_blocks * MOE_ROWS, d), F32),
        grid_spec=pltpu.PrefetchScalarGridSpec(
            num_scalar_prefetch=2,
            grid=(n_blocks,),
            in_specs=[pl.BlockSpec((MOE_ROWS, d), lambda b, be, nr: (jnp.minimum(b, nr[0] - 1), 0)),
                      wspec(w1), wspec(b1), wspec(w2), wspec(b2)],
            out_specs=pl.BlockSpec((MOE_ROWS, d), lambda b, be, nr: (b, 0)),
            scratch_shapes=[pltpu.VMEM((d, 2 * dff), BF16), pltpu.VMEM((d // LANES, dff, LANES), F32),
                            pltpu.VMEM((dff, d), BF16)],
        ),
        compiler_params=_cparams(("arbitrary",)),
        name="moe_experts",
    )(blk_e, n_real, xs, w1, b1, w2, b2)


def _combine_kernel(pos_ref, ys_hbm, x1_ref, tg_ref, p_ref, wple_ref, wpg_ref, gple_ref, gfin_ref,
                    yh_ref, yt_ref, buf, sem, *, nbh):
    i = pl.program_id(0)
    tm = x1_ref.shape[0]

    def row_copy(blk, slot, r, kk):
        src = pos_ref[(blk * tm + r) * TOP_K + kk]
        return pltpu.make_async_copy(ys_hbm.at[pl.ds(src, 1), :], buf.at[slot, kk, pl.ds(r, 1), :],
                                     sem.at[slot])

    @pl.when(i == 0)
    def _():
        def body(r, carry):
            for kk in range(TOP_K):
                row_copy(0, 0, r, kk).start()
            return carry
        lax.fori_loop(0, tm, body, 0)

    def run(prefetch):
        slot = i % 2
        for kk in range(TOP_K):
            pltpu.make_async_copy(ys_hbm.at[pl.ds(0, tm), :], buf.at[slot, kk], sem.at[slot]).wait()
        if prefetch:
            for r in range(tm):
                for kk in range(TOP_K):
                    row_copy(i + 1, 1 - slot, r, kk).start()
        tg = tg_ref[...]
        x2 = x1_ref[...]
        for kk in range(TOP_K):
            x2 = x2 + tg[:, kk:kk + 1] * buf[slot, kk]
        gate = _sigmoid(_dot(_rms(x2, gple_ref[...]).astype(BF16), wpg_ref[...]))
        x3 = x2 + _dot(p_ref[...].astype(BF16), wple_ref[...]) * gate
        y = _rms(x3, gfin_ref[...])

        @pl.when(i < nbh)
        def _():
            yh_ref[...] = y

        @pl.when(i >= nbh)
        def _():
            yt_ref[...] = y

    @pl.when(i + 1 < pl.num_programs(0))
    def _():
        run(True)

    @pl.when(i + 1 == pl.num_programs(0))
    def _():
        run(False)


def _combine(pos, ys, x1, tg, p_all, w_ple, w_pg, g_ple, g_fin, tm, n_head):
    n, d = x1.shape
    nbh = n_head // tm
    row = lambda i, ps: (i, 0)
    full = lambda a: pl.BlockSpec(a.shape, lambda i, ps: (0, 0))
    return pl.pallas_call(
        functools.partial(_combine_kernel, nbh=nbh),
        out_shape=(jax.ShapeDtypeStruct((n_head, d), F32), jax.ShapeDtypeStruct((n - n_head, d), F32)),
        grid_spec=pltpu.PrefetchScalarGridSpec(
            num_scalar_prefetch=1,
            grid=(n // tm,),
            in_specs=[pl.BlockSpec(memory_space=pl.ANY),
                      pl.BlockSpec((tm, d), row), pl.BlockSpec((tm, LANES), row),
                      pl.BlockSpec((tm, p_all.shape[1]), row),
                      full(w_ple), full(w_pg), full(g_ple), full(g_fin)],
            out_specs=(pl.BlockSpec((tm, d), lambda i, ps: (jnp.minimum(i, nbh - 1), 0)),
                       pl.BlockSpec((tm, d), lambda i, ps: (jnp.maximum(i - nbh, 0), 0))),
            scratch_shapes=[pltpu.VMEM((2, TOP_K, tm, d), F32), pltpu.SemaphoreType.DMA((2,))],
        ),
        compiler_params=_cparams(("arbitrary",)),
        name="combine_ple",
    )(pos, ys, x1, tg, p_all, w_ple, w_pg, g_ple, g_fin)


def _routing(top_i, n_tok):
    a = n_tok * TOP_K
    flat_e = top_i.reshape(a)
    onehot = (flat_e[:, None] == jnp.arange(N_EXPERTS, dtype=jnp.int32)[None, :]).astype(jnp.int32)
    rank = jnp.take_along_axis(jnp.cumsum(onehot, axis=0), flat_e[:, None], axis=1)[:, 0] - 1
    counts = jnp.sum(onehot, axis=0)
    padded = (counts + MOE_ROWS - 1) // MOE_ROWS * MOE_ROWS
    pad_end = jnp.cumsum(padded)
    pad_start = pad_end - padded
    dest = (pad_start[flat_e] + rank).astype(jnp.int32)
    cnt_end = (pad_start + counts).astype(jnp.int32)
    n_blocks = -(-a // MOE_ROWS) + N_EXPERTS
    starts = jnp.arange(n_blocks, dtype=jnp.int32) * MOE_ROWS
    blk_e = jnp.minimum(jnp.sum((pad_end[None, :] <= starts[:, None]).astype(jnp.int32), axis=1),
                        N_EXPERTS - 1).astype(jnp.int32)
    n_real = (pad_end[-1] // MOE_ROWS).astype(jnp.int32).reshape(1)
    return dest, cnt_end, pad_end.astype(jnp.int32), blk_e, n_real, n_blocks


def _pick_tile(n, prefs):
    for t in prefs:
        if n % t == 0:
            return t
    raise ValueError(f"no tile in {prefs} divides {n}")


def kernel(x_prompt, x_sample, cache_ckv, cache_kpe, state_hgrn, page_table, p_prompt, p_sample, hg_lb, g_mix, w_in, g_qnorm, w_uq, w_uk, w_uv, g_kvnorm, g_onorm, w_pa, w_pb, w_out, g_ffn, w_router, b_router, w1, b1, w2, b2, g_ple, w_ple, w_pg, g_final):
    n_p, t_p, d = x_prompt.shape
    n_s, t_s, _ = x_sample.shape
    depth = w_in.shape[0]
    assert depth == 1 and n_p == 1
    n_pages = page_table.shape[1]
    page = cache_ckv.shape[2]
    past = n_pages * page
    tok_p = n_p * t_p
    tok_s = n_s * t_s
    n_tok = tok_p + tok_s
    assert tok_p % Q_TILE == 0 and tok_s % Q_TILE == 0 and Q_TILE % t_s == 0
    qw = HG_HEADS * HG_DK
    vw = HG_HEADS * HG_DV
    row2 = lambda v: v.reshape(1, -1).astype(F32)

    lb = jnp.cumsum(jax.nn.softmax(hg_lb.astype(F32), axis=0), axis=0)[0].reshape(1, qw)
    wi = w_in[0]
    hg_cols = 2 * qw + 2 * vw
    mla_cols = MLA_Q_RANK + MLA_KV_RANK + MLA_ROPE
    w_h = wi[:, :hg_cols].astype(BF16)
    w_mla = wi[:, hg_cols:hg_cols + mla_cols].astype(BF16)
    w_g = wi[:, hg_cols + mla_cols:].astype(BF16)
    wq = w_uq[0].reshape(MLA_Q_RANK, MLA_HEADS, MLA_NOPE + MLA_ROPE)
    w_uqt = jnp.concatenate([wq[:, :, :MLA_NOPE].reshape(MLA_Q_RANK, -1),
                             wq[:, :, MLA_NOPE:].reshape(MLA_Q_RANK, -1)], axis=1).T.astype(BF16)
    w_ukh = jnp.transpose(w_uk[0], (1, 0, 2)).astype(BF16)
    w_uvh = jnp.transpose(w_uv[0], (1, 0, 2)).astype(BF16)
    w_uvt = jnp.transpose(w_uv[0], (1, 2, 0)).astype(BF16)
    w_r = jnp.zeros((d, LANES), F32).at[:, :N_EXPERTS].set(w_router[0].astype(F32))
    w_r3 = jnp.stack(_split3(w_r))
    b_r = jnp.full((1, LANES), -1e30, F32).at[0, :N_EXPERTS].set(b_router[0].astype(F32))

    half = MLA_ROPE // 2
    inv = ROPE_THETA ** (-np.arange(half, dtype=np.float64) / half)
    pos = np.concatenate([np.tile(np.arange(t_p), n_p), np.tile(past + np.arange(t_s), n_s)])
    ang = pos.astype(np.float64)[:, None] * inv[None, :]
    cos, sin = np.cos(ang).astype(np.float32), np.sin(ang).astype(np.float32)
    cos2 = np.concatenate([cos, cos], axis=1)
    sin2 = np.concatenate([-sin, sin], axis=1)

    x_all = jnp.concatenate([x_prompt.reshape(tok_p, d), x_sample.reshape(tok_s, d)], axis=0)
    p_all = jnp.concatenate([p_prompt[0].reshape(tok_p, -1), p_sample[0].reshape(tok_s, -1)], axis=0)

    tm = _pick_tile(n_tok, (512, 256, 128))
    zh, gates, c_new, kpe_new, qlat_t, qpe_t, kcb, kpeb = _inproj(
        x_all, cos2, sin2, cos.T, sin.T, row2(g_mix[0]), w_h, w_mla, w_g, row2(g_qnorm[0]), w_uqt, w_ukh,
        row2(g_kvnorm[0]), tm)

    gon = row2(g_onorm[0])
    chunk_p = 64 if t_p % 64 == 0 else t_p
    step_p = _pick_tile(t_p, (4 * chunk_p, 2 * chunk_p, chunk_p))
    oa_p, st_p = _hgrn(zh[None], t_p, lb, gon,
                       jnp.zeros((n_p, HG_HEADS, HG_DK, HG_DV), F32), chunk_p, chunk_p, step_p)
    t_pad = -(-t_s // SUBLANES) * SUBLANES
    zh_s = jnp.pad(zh[tok_p:].reshape(n_s, t_s, -1), ((0, 0), (0, t_pad - t_s), (0, 0)))
    oa_s, st_s = _hgrn(zh_s, t_pad, lb, gon, state_hgrn[0].astype(F32), t_pad, t_s, t_pad)
    o_a = jnp.concatenate([oa_p.reshape(tok_p, vw), oa_s[:, :t_s].reshape(tok_s, vw)], axis=0)

    nq_p = tok_p // Q_TILE
    tk = _pick_tile(tok_p, (512, 256, 128))
    ob_p = _attn_prompt(qlat_t, qpe_t, kcb, kpeb, kcb[:tok_p].T, w_uvt, tk)

    def sample_rows(a):
        w = a.shape[1]
        a = a.reshape(-1, w, MLA_HEADS, Q_TILE // t_s, t_s)
        return jnp.transpose(a, (0, 3, 2, 4, 1)).reshape(n_s, MLA_HEADS * t_s, w).astype(F32)

    def new_keys(a):
        return jnp.pad(a.reshape(n_s, t_s, -1), ((0, 0), (0, NEW_PAD - t_s), (0, 0)))

    ob_s = _attn_sample(page_table, sample_rows(qlat_t[nq_p:]), sample_rows(qpe_t[nq_p:]),
                        cache_ckv[0], jnp.swapaxes(cache_kpe[0], 1, 2),
                        new_keys(c_new[tok_p:]), new_keys(kpe_new[tok_p:]), w_uvh, t_s)
    o_b = jnp.concatenate([ob_p, ob_s.reshape(tok_s, -1)], axis=0)

    x1, h2, top_i, top_g = _merge(o_a, o_b, gates, x_all, w_pa[0].astype(BF16), w_pb[0].astype(BF16),
                                  w_out[0].astype(BF16), row2(g_ffn[0]), w_r3, b_r, tm)

    dest, cnt_end, pad_end, blk_e, n_real, n_blocks = _routing(top_i[:, :TOP_K], n_tok)
    tc = _pick_tile(n_tok, (256, 128))
    xs = _dispatch(dest, cnt_end, pad_end, n_real, h2, n_blocks * MOE_ROWS, tm)
    ys = _moe(blk_e, n_real, xs, w1[0].astype(F32), b1[0][:, None, :].astype(F32),
              w2[0].astype(F32), b2[0][:, None, :].astype(F32))

    y_p, y_s = _combine(dest, ys, x1, top_g, p_all, w_ple[0].astype(BF16), w_pg[0].astype(BF16),
                        row2(g_ple[0]), row2(g_final), tc, tok_p)

    y_prompt = y_p.reshape(n_p, t_p, d)
    y_sample = y_s.reshape(n_s, t_s, d)
    return (y_prompt, y_sample,
            c_new[:tok_p].reshape(1, n_p, t_p, -1), kpe_new[:tok_p].reshape(1, n_p, t_p, -1), st_p[None],
            c_new[tok_p:].reshape(1, n_s, t_s, -1), kpe_new[tok_p:].reshape(1, n_s, t_s, -1), st_s[None])
```

```python
import functools
import math

import numpy as np
import jax
import jax.numpy as jnp
from jax import lax
from jax.experimental import pallas as pl
from jax.experimental.pallas import tpu as pltpu

F32 = jnp.float32
BF16 = jnp.bfloat16

HG_HEADS = 4
HG_DK = 128
HG_DV = 128
MLA_HEADS = 4
MLA_NOPE = 128
MLA_ROPE = 64
MLA_V = 128
MLA_Q_RANK = 384
MLA_KV_RANK = 256
MLA_SCALE = (MLA_NOPE + MLA_ROPE) ** -0.5
Q_SCALE = MLA_SCALE * math.log2(math.e)
ROPE_THETA = 10000.0
N_EXPERTS = 32
TOP_K = 4
SWIGLU_ALPHA = 1.702
SWIGLU_LIMIT = 7.0
EPS = 1e-6

LANES = 128
SUBLANES = 8
VMEM_LIMIT = 56 * 1024 * 1024

Q_TILE = 256
MOE_ROWS = 512
NEG_INF = float("-inf")


def _cparams(sem):
    return pltpu.CompilerParams(dimension_semantics=sem, vmem_limit_bytes=VMEM_LIMIT)


def _dot(a, b):
    return jnp.dot(a, b, preferred_element_type=F32)


def _dot_nt(a, b):
    return lax.dot_general(a, b, (((1,), (1,)), ((), ())), preferred_element_type=F32)


def _dot_tn(a, b):
    return lax.dot_general(a, b, (((0,), (0,)), ((), ())), preferred_element_type=F32)


def _rms(x, g):
    return x * lax.rsqrt(jnp.mean(x * x, axis=-1, keepdims=True) + EPS) * g


def _sigmoid(x):
    return 1.0 / (1.0 + jnp.exp(-x))


def _split3(x):
    hi = x.astype(BF16)
    r1 = x - hi.astype(F32)
    mid = r1.astype(BF16)
    lo = (r1 - mid.astype(F32)).astype(BF16)
    return hi, mid, lo


def _rope64(v, cos2, sin2):
    half = MLA_ROPE // 2
    partner = jnp.concatenate([v[:, half:], v[:, :half]], axis=1)
    return v * cos2 + partner * sin2


def _head_tail_specs(head, tail, tm):
    nbh = head.shape[0] // tm
    assert head.shape[0] % tm == 0 and tail.shape[0] % tm == 0
    return [pl.BlockSpec((tm, head.shape[1]), lambda i, *_: (jnp.minimum(i, nbh - 1), 0)),
            pl.BlockSpec((tm, tail.shape[1]), lambda i, *_: (jnp.maximum(i - nbh, 0), 0))]


def _head_tail_block(head_ref, tail_ref, nbh):
    return jnp.where(pl.program_id(0) < nbh, head_ref[...], tail_ref[...])


def _inproj_kernel(xh_ref, xt_ref, cos_ref, sin_ref, cost_ref, sint_ref, gmix_ref, wh_ref, wmla_ref, wg_ref,
                   gq_ref, wuqt_ref, wuk_ref, gkv_ref,
                   zh_ref, gates_ref, c_ref, kpe_ref, qlat_ref, qpe_ref, kcb_ref, kpeb_ref, *, nbh):
    tm = xh_ref.shape[0]
    half = MLA_ROPE // 2
    h = _rms(_head_tail_block(xh_ref, xt_ref, nbh), gmix_ref[...]).astype(BF16)
    zh_ref[...] = _dot(h, wh_ref[...])
    gates_ref[...] = _sigmoid(_dot(h, wg_ref[...]))
    zm = _dot(h, wmla_ref[...])
    c = _rms(zm[:, MLA_Q_RANK:MLA_Q_RANK + MLA_KV_RANK], gkv_ref[...])
    c_ref[...] = c
    kcb_ref[...] = c.astype(BF16)
    kpe = _rope64(zm[:, MLA_Q_RANK + MLA_KV_RANK:], cos_ref[...], sin_ref[...])
    kpe_ref[...] = kpe
    kpeb_ref[...] = kpe.astype(BF16)
    qn_t = _rms(zm[:, :MLA_Q_RANK], gq_ref[...]).T.astype(BF16)
    q_t = _dot(wuqt_ref[...], qn_t)
    cos_t = cost_ref[...]
    sin_t = sint_ref[...]
    for hh in range(MLA_HEADS):
        q_nope = q_t[hh * MLA_NOPE:(hh + 1) * MLA_NOPE].astype(BF16)
        qlat = (_dot(wuk_ref[hh], q_nope) * Q_SCALE).astype(BF16)
        off = MLA_HEADS * MLA_NOPE + hh * MLA_ROPE
        x1 = q_t[off:off + half]
        x2 = q_t[off + half:off + MLA_ROPE]
        qpe = (jnp.concatenate([x1 * cos_t - x2 * sin_t, x1 * sin_t + x2 * cos_t], axis=0)
               * Q_SCALE).astype(BF16)
        for tb in range(tm // Q_TILE):
            qlat_ref[tb, :, hh * Q_TILE:(hh + 1) * Q_TILE] = qlat[:, tb * Q_TILE:(tb + 1) * Q_TILE]
            qpe_ref[tb, :, hh * Q_TILE:(hh + 1) * Q_TILE] = qpe[:, tb * Q_TILE:(tb + 1) * Q_TILE]


def _inproj(x_head, x_tail, cos2, sin2, cos_t, sin_t, g_mix, w_h, w_mla, w_g, g_q, w_uqt, w_ukh, g_kv, tm):
    n, d = x_head.shape[0] + x_tail.shape[0], x_head.shape[1]
    nq = n // Q_TILE
    half = MLA_ROPE // 2
    rows = MLA_HEADS * Q_TILE
    row = lambda i: (i, 0)
    col = lambda i: (0, i)
    const2 = lambda i: (0, 0)
    const3 = lambda i: (0, 0, 0)
    full = lambda a: pl.BlockSpec(a.shape, const2 if a.ndim == 2 else const3)
    out_shape = (
        jax.ShapeDtypeStruct((n, w_h.shape[1]), F32),
        jax.ShapeDtypeStruct((n, w_g.shape[1]), F32),
        jax.ShapeDtypeStruct((n, MLA_KV_RANK), F32),
        jax.ShapeDtypeStruct((n, MLA_ROPE), F32),
        jax.ShapeDtypeStruct((nq, MLA_KV_RANK, rows), BF16),
        jax.ShapeDtypeStruct((nq, MLA_ROPE, rows), BF16),
        jax.ShapeDtypeStruct((n, MLA_KV_RANK), BF16),
        jax.ShapeDtypeStruct((n, MLA_ROPE), BF16),
    )
    tq = tm // Q_TILE
    out_specs = (
        pl.BlockSpec((tm, w_h.shape[1]), row),
        pl.BlockSpec((tm, w_g.shape[1]), row),
        pl.BlockSpec((tm, MLA_KV_RANK), row),
        pl.BlockSpec((tm, MLA_ROPE), row),
        pl.BlockSpec((tq, MLA_KV_RANK, rows), lambda i: (i, 0, 0)),
        pl.BlockSpec((tq, MLA_ROPE, rows), lambda i: (i, 0, 0)),
        pl.BlockSpec((tm, MLA_KV_RANK), row),
        pl.BlockSpec((tm, MLA_ROPE), row),
    )
    in_specs = _head_tail_specs(x_head, x_tail, tm) + [
        pl.BlockSpec((tm, MLA_ROPE), row),
        pl.BlockSpec((tm, MLA_ROPE), row),
        pl.BlockSpec((half, tm), col),
        pl.BlockSpec((half, tm), col),
        full(g_mix), full(w_h), full(w_mla), full(w_g), full(g_q), full(w_uqt), full(w_ukh), full(g_kv),
    ]
    return pl.pallas_call(
        functools.partial(_inproj_kernel, nbh=x_head.shape[0] // tm),
        out_shape=out_shape,
        grid=(n // tm,),
        in_specs=in_specs,
        out_specs=out_specs,
        compiler_params=_cparams(("parallel",)),
        name="inproj",
    )(x_head, x_tail, cos2, sin2, cos_t, sin_t, g_mix, w_h, w_mla, w_g, g_q, w_uqt, w_ukh, g_kv)


HG_DIAG = 8


def _hgrn_levels(chunk):
    levels, m = [], chunk // 2
    while m >= HG_DIAG:
        levels.append(m)
        m //= 2
    return tuple(levels)


def _hgrn_mats(chunk):
    r = np.arange(chunk)[:, None]
    j = np.arange(chunk)[None, :]
    mats = [j <= r, j > r]
    for m in _hgrn_levels(chunk):
        same = (r // (2 * m)) == (j // (2 * m))
        second = (r % (2 * m)) >= m
        mid = (r // (2 * m)) * 2 * m + m
        a = second & same & (j >= mid) & (j <= r)
        b = (~second) & same & (j > r) & (j < mid)
        mats.append(a | b)
    return np.concatenate(mats, axis=0).astype(np.float32)


def _hgrn_kernel(zh_ref, mats_ref, lb_ref, gon_ref, s0_ref, o_ref, sout_ref, st_ref, *, chunk, t_valid):
    ci = pl.program_id(1)

    @pl.when(ci == 0)
    def _():
        for h in range(HG_HEADS):
            st_ref[h] = s0_ref[0, h].T

    for r0 in range(0, zh_ref.shape[1], chunk):
        _hgrn_chunk(zh_ref, mats_ref, lb_ref, gon_ref, o_ref, st_ref, r0, chunk, t_valid)

    @pl.when(ci == pl.num_programs(1) - 1)
    def _():
        for h in range(HG_HEADS):
            sout_ref[0, h] = st_ref[h].T


def _hgrn_chunk(zh_ref, mats_ref, lb_ref, gon_ref, o_ref, st_ref, r0, chunk, t_valid):
    levels = _hgrn_levels(chunk)
    qw = HG_HEADS * HG_DK
    zh = zh_ref[0, r0:r0 + chunk]
    zq, zf, zi, zg = zh[:, :qw], zh[:, qw:2 * qw], zh[:, 2 * qw:3 * qw], zh[:, 3 * qw:]
    lb = lb_ref[...]
    f = lb + (1.0 - lb) * _sigmoid(zf)
    k = 1.0 - f
    logf = jnp.log(f)
    row = lax.broadcasted_iota(jnp.int32, (chunk, 1), 0)
    if t_valid < chunk:
        live = row < t_valid
        logf = jnp.where(live, logf, 0.0)
        k = jnp.where(live, k, 0.0)
    hi, mid, lo = _split3(logf)
    mats = mats_ref[...]
    e_all = _dot(mats, hi) + _dot(mats, mid) + _dot(mats, lo)
    b = e_all[:chunk]
    e_end = e_all[chunk:2 * chunk]
    b_end = b[chunk - 1:chunk, :]
    q = zq * (HG_DK ** -0.5)
    rr = lax.broadcasted_iota(jnp.int32, (chunk, chunk), 0)
    cc = lax.broadcasted_iota(jnp.int32, (chunk, chunk), 1)
    gon = gon_ref[...]

    for h in range(HG_HEADS):
        hs = slice(h * HG_DK, (h + 1) * HG_DK)
        qh, kh, vh, bh = q[:, hs], k[:, hs], zi[:, hs], b[:, hs]
        st = st_ref[h]
        o = _dot_nt((qh * jnp.exp(bh)).astype(BF16), st.astype(BF16))
        if levels:
            att = jnp.zeros((chunk, chunk), F32)
            for li, m in enumerate(levels):
                w = jnp.exp(e_all[(2 + li) * chunk:(3 + li) * chunk, hs])
                second = (row % (2 * m)) >= m
                qm = jnp.where(second, qh * w, 0.0).astype(BF16)
                km = jnp.where(second, 0.0, kh * w).astype(BF16)
                a = _dot_nt(qm, km)
                att = att + jnp.where((rr // (2 * m)) == (cc // (2 * m)), a, 0.0)
            o = o + _dot(att.astype(BF16), vh.astype(BF16))
        for dlt in range(min(HG_DIAG, chunk)):
            if dlt == 0:
                a = jnp.sum(qh * kh, axis=-1, keepdims=True)
                o = o + a * vh
            else:
                bs = pltpu.roll(bh, dlt, 0)
                ks = pltpu.roll(kh, dlt, 0)
                vs = pltpu.roll(vh, dlt, 0)
                w = jnp.where((row % HG_DIAG) >= dlt, jnp.exp(bh - bs), 0.0)
                a = jnp.sum(qh * ks * w, axis=-1, keepdims=True)
                o = o + a * vs
        on = _rms(o, gon)
        zgh = zg[:, hs]
        o_ref[0, r0:r0 + chunk, hs] = (on * (zgh * _sigmoid(zgh))).astype(o_ref.dtype)
        ke = (kh * jnp.exp(e_end[:, hs])).astype(BF16)
        st_ref[h] = st * jnp.exp(b_end[:, hs]) + _dot_tn(vh.astype(BF16), ke)


def _hgrn(zh, t, lb, g_onorm, s0, chunk, t_valid, step):
    n, _, w = zh.shape
    mats = jnp.asarray(_hgrn_mats(chunk), BF16)
    kern = functools.partial(_hgrn_kernel, chunk=chunk, t_valid=t_valid)
    vw = HG_HEADS * HG_DV
    return pl.pallas_call(
        kern,
        out_shape=(jax.ShapeDtypeStruct((n, t, vw), BF16),
                   jax.ShapeDtypeStruct(s0.shape, F32)),
        grid=(n, t // step),
        in_specs=[
            pl.BlockSpec((1, step, w), lambda i, c: (i, c, 0)),
            pl.BlockSpec(mats.shape, lambda i, c: (0, 0)),
            pl.BlockSpec(lb.shape, lambda i, c: (0, 0)),
            pl.BlockSpec(g_onorm.shape, lambda i, c: (0, 0)),
            pl.BlockSpec((1,) + s0.shape[1:], lambda i, c: (i, 0, 0, 0)),
        ],
        out_specs=(
            pl.BlockSpec((1, step, vw), lambda i, c: (i, c, 0)),
            pl.BlockSpec((1,) + s0.shape[1:], lambda i, c: (i, 0, 0, 0)),
        ),
        scratch_shapes=[pltpu.VMEM((HG_HEADS, HG_DV, HG_DK), F32)],
        compiler_params=_cparams(("arbitrary", "arbitrary")),
        name="hgrn2",
    )(zh, mats, lb, g_onorm, s0)


def _attn_prompt_kernel(qlat_ref, qpe_ref, kc_ref, kpe_ref, kct_ref, wuvt_ref, o_ref,
                        s0_ref, s1_ref, m_ref, l_ref, acc_ref, *, tk):
    i = pl.program_id(0)
    rows = MLA_HEADS * Q_TILE
    n_blk = (i * Q_TILE) // tk + 1
    acc_ref[...] = jnp.zeros(acc_ref.shape, F32)

    def scores(j, s_ref):
        off = pl.multiple_of(j * tk, tk)
        s_ref[...] = (_dot(kc_ref[pl.ds(off, tk), :], qlat_ref[0])
                      + _dot(kpe_ref[pl.ds(off, tk), :], qpe_ref[0]))

    def consume(j, s_ref, m_old, l_old, masked):
        off = pl.multiple_of(j * tk, tk)
        s = s_ref[...]
        if masked:
            k_pos = off + lax.broadcasted_iota(jnp.int32, s.shape, 0)
            q_pos = i * Q_TILE + lax.broadcasted_iota(jnp.int32, s.shape, 1) % Q_TILE
            s = jnp.where(k_pos <= q_pos, s, NEG_INF)
        m_new = jnp.maximum(m_old, jnp.max(s, axis=0, keepdims=True))
        p = jnp.exp2(s - m_new)
        alpha = jnp.exp2(m_old - m_new)
        l_new = alpha * l_old + jnp.sum(p, axis=0, keepdims=True)
        acc_ref[...] = alpha * acc_ref[...] + _dot(kct_ref[:, pl.ds(off, tk)], p.astype(BF16))
        return m_new, l_new

    scores(0, s0_ref)
    n_pairs = (n_blk - 1) // 2

    def body(t, carry):
        m, l = carry
        scores(2 * t + 1, s1_ref)
        m, l = consume(2 * t, s0_ref, m, l, False)
        scores(2 * t + 2, s0_ref)
        m, l = consume(2 * t + 1, s1_ref, m, l, False)
        return m, l

    m, l = lax.fori_loop(0, n_pairs, body,
                         (jnp.full((1, rows), NEG_INF, F32), jnp.zeros((1, rows), F32)))
    m_ref[...] = m
    l_ref[...] = l
    last = n_blk - 1

    @pl.when(2 * n_pairs == last)
    def _():
        _, l2 = consume(last, s0_ref, m_ref[...], l_ref[...], True)
        l_ref[...] = l2

    @pl.when(2 * n_pairs != last)
    def _():
        scores(last, s1_ref)
        m1, l1 = consume(last - 1, s0_ref, m_ref[...], l_ref[...], False)
        _, l2 = consume(last, s1_ref, m1, l1, True)
        l_ref[...] = l2

    o_lat = (acc_ref[...] / l_ref[...]).astype(BF16)
    for h in range(MLA_HEADS):
        ob_t = _dot(wuvt_ref[h], o_lat[:, h * Q_TILE:(h + 1) * Q_TILE])
        o_ref[:, h * MLA_V:(h + 1) * MLA_V] = ob_t.T.astype(o_ref.dtype)


def _attn_prompt(qlat_t, qpe_t, kc, kpe, kct, w_uvt, tk):
    t = kct.shape[1]
    nq = t // Q_TILE
    rows = MLA_HEADS * Q_TILE
    kern = functools.partial(_attn_prompt_kernel, tk=tk)
    resident = lambda shape: pl.BlockSpec(shape, lambda i: (0,) * len(shape), pipeline_mode=pl.Buffered(1))
    return pl.pallas_call(
        kern,
        out_shape=jax.ShapeDtypeStruct((nq * Q_TILE, MLA_HEADS * MLA_V), BF16),
        grid=(nq,),
        in_specs=[
            pl.BlockSpec((1, MLA_KV_RANK, rows), lambda i: (i, 0, 0)),
            pl.BlockSpec((1, MLA_ROPE, rows), lambda i: (i, 0, 0)),
            resident((t, MLA_KV_RANK)),
            resident((t, MLA_ROPE)),
            resident((MLA_KV_RANK, t)),
            resident(w_uvt.shape),
        ],
        out_specs=pl.BlockSpec((Q_TILE, MLA_HEADS * MLA_V), lambda i: (i, 0)),
        scratch_shapes=[pltpu.VMEM((tk, rows), F32), pltpu.VMEM((tk, rows), F32),
                        pltpu.VMEM((1, rows), F32), pltpu.VMEM((1, rows), F32),
                        pltpu.VMEM((MLA_KV_RANK, rows), F32)],
        compiler_params=_cparams(("parallel",)),
        name="attn_prompt",
    )(qlat_t, qpe_t, kc, kpe, kct, w_uvt)


NEW_PAD = 16


def _attn_sample_kernel(pt_ref, q1_ref, q2_ref, ckv_hbm, kpet_hbm, cn_ref, kn_ref, wuv_ref, o_ref,
                        cbuf, pbuf, spe, sem, *, t_new):
    n = pl.program_id(0)
    n_pages = cbuf.shape[1]
    page = cbuf.shape[2]

    def issue(seq, slot):
        def body(k, carry):
            pg = pt_ref[seq, k]
            pltpu.make_async_copy(ckv_hbm.at[pg], cbuf.at[slot, k], sem.at[0, slot]).start()
            pltpu.make_async_copy(kpet_hbm.at[pg], pbuf.at[slot, k], sem.at[1, slot]).start()
            return carry
        lax.fori_loop(0, n_pages, body, 0, unroll=4)

    @pl.when(n == 0)
    def _():
        issue(0, 0)

    @pl.when(n + 1 < pl.num_programs(0))
    def _():
        issue(n + 1, (n + 1) % 2)

    slot = n % 2
    pltpu.make_async_copy(ckv_hbm.at[pl.ds(0, n_pages)], cbuf.at[slot], sem.at[0, slot]).wait()
    pltpu.make_async_copy(kpet_hbm.at[pl.ds(0, n_pages)], pbuf.at[slot], sem.at[1, slot]).wait()

    q1 = q1_ref[0]
    q2 = q2_ref[0]
    c = cbuf[slot].reshape(n_pages * page, MLA_KV_RANK)
    for k in range(n_pages):
        spe[:, k * page:(k + 1) * page] = _dot(q2, pbuf[slot, k])
    s = _dot_nt(q1, c) + spe[...]
    cn = cn_ref[0]
    s2 = _dot_nt(q1, cn) + _dot_nt(q2, kn_ref[0])
    t_q = lax.broadcasted_iota(jnp.int32, s2.shape, 0) % t_new
    t_k = lax.broadcasted_iota(jnp.int32, s2.shape, 1)
    s2 = jnp.where(t_k <= t_q, s2, NEG_INF)
    m = jnp.maximum(jnp.max(s, axis=-1, keepdims=True), jnp.max(s2, axis=-1, keepdims=True))
    p = jnp.exp2(s - m)
    p2 = jnp.exp2(s2 - m)
    l = jnp.sum(p, axis=-1, keepdims=True) + jnp.sum(p2, axis=-1, keepdims=True)
    o_lat = ((_dot(p, c) + _dot(p2, cn)) / l).astype(BF16)
    for h in range(MLA_HEADS):
        o_ref[0, :, h * MLA_V:(h + 1) * MLA_V] = _dot(
            o_lat[h * t_new:(h + 1) * t_new], wuv_ref[h]).astype(o_ref.dtype)


def _attn_sample(page_table, q1, q2, ckv, kpet, cn, kn, w_uv, t_new):
    n, rows, _ = q1.shape
    n_pages = page_table.shape[1]
    page = ckv.shape[1]
    kern = functools.partial(_attn_sample_kernel, t_new=t_new)
    return pl.pallas_call(
        kern,
        out_shape=jax.ShapeDtypeStruct((n, t_new, MLA_HEADS * MLA_V), BF16),
        grid_spec=pltpu.PrefetchScalarGridSpec(
            num_scalar_prefetch=1,
            grid=(n,),
            in_specs=[
                pl.BlockSpec((1, rows, MLA_KV_RANK), lambda i, pt: (i, 0, 0)),
                pl.BlockSpec((1, rows, MLA_ROPE), lambda i, pt: (i, 0, 0)),
                pl.BlockSpec(memory_space=pl.ANY),
                pl.BlockSpec(memory_space=pl.ANY),
                pl.BlockSpec((1, NEW_PAD, MLA_KV_RANK), lambda i, pt: (i, 0, 0)),
                pl.BlockSpec((1, NEW_PAD, MLA_ROPE), lambda i, pt: (i, 0, 0)),
                pl.BlockSpec(w_uv.shape, lambda i, pt: (0, 0, 0)),
            ],
            out_specs=pl.BlockSpec((1, t_new, MLA_HEADS * MLA_V), lambda i, pt: (i, 0, 0)),
            scratch_shapes=[pltpu.VMEM((2, n_pages, page, MLA_KV_RANK), F32),
                            pltpu.VMEM((2, n_pages, MLA_ROPE, page), F32),
                            pltpu.VMEM((rows, n_pages * page), F32),
                            pltpu.SemaphoreType.DMA((2, 2))],
        ),
        compiler_params=_cparams(("arbitrary",)),
        name="attn_sample",
    )(page_table, q1, q2, ckv, kpet, cn, kn, w_uv)


def _merge_kernel(oah_ref, oat_ref, obh_ref, obt_ref, xh_ref, xt_ref, gates_ref, wpa_ref, wpb_ref, wout_ref,
                  gffn_ref, wr_ref, br_ref, x1_ref, h2_ref, ti_ref, tg_ref, *, nbh):
    d = xh_ref.shape[1]
    gates = gates_ref[...]
    o_a = _head_tail_block(oah_ref, oat_ref, nbh)
    o_b = _head_tail_block(obh_ref, obt_ref, nbh)
    mix = gates[:, :d] * _dot(o_a, wpa_ref[...]) + gates[:, d:] * _dot(o_b, wpb_ref[...])
    x1 = _head_tail_block(xh_ref, xt_ref, nbh) + _dot(mix.astype(BF16), wout_ref[...])
    x1_ref[...] = x1
    h2 = _rms(x1, gffn_ref[...])
    h2_ref[...] = h2
    h_hi, h_mid, h_lo = _split3(h2)
    w_hi, w_mid, w_lo = wr_ref[0], wr_ref[1], wr_ref[2]
    logits = (_dot(h_hi, w_hi) + (_dot(h_hi, w_mid) + _dot(h_mid, w_hi))
              + (_dot(h_mid, w_mid) + _dot(h_hi, w_lo) + _dot(h_lo, w_hi))) + br_ref[...]
    lane = lax.broadcasted_iota(jnp.int32, logits.shape, 1)
    vals, idxs = [], []
    cur = logits
    for _ in range(TOP_K):
        mx = jnp.max(cur, axis=-1, keepdims=True)
        ix = jnp.min(jnp.where(cur == mx, lane, LANES), axis=-1, keepdims=True)
        vals.append(mx)
        idxs.append(ix)
        cur = jnp.where(lane == ix, NEG_INF, cur)
    es = [jnp.exp(v - vals[0]) for v in vals]
    tot = es[0] + es[1] + es[2] + es[3]
    ti = jnp.zeros(logits.shape, jnp.int32)
    tg = jnp.zeros(logits.shape, F32)
    for kk in range(TOP_K):
        ti = jnp.where(lane == kk, idxs[kk], ti)
        tg = jnp.where(lane == kk, es[kk] / tot, tg)
    ti_ref[...] = ti
    tg_ref[...] = tg


def _merge(oa_ht, ob_ht, x_ht, gates, w_pa, w_pb, w_out, g_ffn, w_r3, b_r, tm):
    n, d = gates.shape[0], x_ht[0].shape[1]
    row = lambda i: (i, 0)
    full = lambda a: pl.BlockSpec(a.shape, (lambda i: (0, 0)) if a.ndim == 2 else (lambda i: (0, 0, 0)))
    return pl.pallas_call(
        functools.partial(_merge_kernel, nbh=x_ht[0].shape[0] // tm),
        out_shape=(jax.ShapeDtypeStruct((n, d), F32), jax.ShapeDtypeStruct((n, d), F32),
                   jax.ShapeDtypeStruct((n, LANES), jnp.int32), jax.ShapeDtypeStruct((n, LANES), F32)),
        grid=(n // tm,),
        in_specs=(_head_tail_specs(*oa_ht, tm) + _head_tail_specs(*ob_ht, tm) + _head_tail_specs(*x_ht, tm)
                  + [pl.BlockSpec((tm, gates.shape[1]), row),
                     full(w_pa), full(w_pb), full(w_out), full(g_ffn), full(w_r3), full(b_r)]),
        out_specs=(pl.BlockSpec((tm, d), row), pl.BlockSpec((tm, d), row),
                   pl.BlockSpec((tm, LANES), row), pl.BlockSpec((tm, LANES), row)),
        compiler_params=_cparams(("parallel",)),
        name="merge_router",
    )(*oa_ht, *ob_ht, *x_ht, gates, w_pa, w_pb, w_out, g_ffn, w_r3, b_r)


def _dispatch_kernel(dest_ref, cend_ref, pend_ref, nreal_ref, h2_ref, xs_ref, zblk, sem, zsem):
    i = pl.program_id(0)
    tm = h2_ref.shape[0]
    blk = zblk.shape[0]

    @pl.when(i == 0)
    def _():
        zblk[...] = jnp.zeros(zblk.shape, zblk.dtype)

        def zero_row(r):
            return pltpu.make_async_copy(zblk.at[pl.ds(0, 1), :], xs_ref.at[pl.ds(r, 1), :], zsem)

        def zero_rows(r, size):
            return pltpu.make_async_copy(zblk.at[pl.ds(0, size), :],
                                         xs_ref.at[pl.ds(pl.multiple_of(r, SUBLANES), size), :], zsem)

        def run(lo, hi, copy, go):
            def body(r, carry):
                go(copy(r))
                return carry
            lax.fori_loop(lo, hi, body, 0)

        sizes = [1 << k for k in range(blk.bit_length() - 1, SUBLANES.bit_length() - 2, -1)]
        for go in (lambda c: c.start(), lambda c: c.wait()):
            for e in range(N_EXPERTS):
                lo = cend_ref[e]
                aligned = jnp.minimum((lo + SUBLANES - 1) // SUBLANES * SUBLANES, pend_ref[e])
                run(lo, aligned, zero_row, go)
                left = pend_ref[e] - aligned
                off = aligned
                for size in sizes:
                    take = (left & size) != 0

                    @pl.when(take)
                    def _():
                        go(zero_rows(off, size))
                    off = off + jnp.where(take, size, 0)
            run(nreal_ref[0], xs_ref.shape[0] // blk, lambda j: zero_rows(j * blk, blk), go)

    def body(r, carry):
        for kk in range(TOP_K):
            dst = dest_ref[(i * tm + r) * TOP_K + kk]
            pltpu.make_async_copy(h2_ref.at[pl.ds(r, 1), :], xs_ref.at[pl.ds(dst, 1), :], sem).start()
        return carry
    lax.fori_loop(0, tm, body, 0, unroll=4)
    for kk in range(TOP_K):
        pltpu.make_async_copy(h2_ref, xs_ref.at[pl.ds(0, tm), :], sem).wait()


def _dispatch(dest, cnt_end, pad_end, n_real, h2, n_rows, tm):
    n, d = h2.shape
    return pl.pallas_call(
        _dispatch_kernel,
        out_shape=jax.ShapeDtypeStruct((n_rows, d), h2.dtype),
        grid_spec=pltpu.PrefetchScalarGridSpec(
            num_scalar_prefetch=4,
            grid=(n // tm,),
            in_specs=[pl.BlockSpec((tm, d), lambda i, ds, ce, pe, nr: (i, 0))],
            out_specs=pl.BlockSpec(memory_space=pl.ANY),
            scratch_shapes=[pltpu.VMEM((MOE_ROWS, d), h2.dtype), pltpu.SemaphoreType.DMA(()),
                            pltpu.SemaphoreType.DMA(())],
        ),
        compiler_params=_cparams(("arbitrary",)),
        name="moe_dispatch",
    )(dest, cnt_end, pad_end, n_real, h2)


def _moe_kernel(blke_ref, nreal_ref, xs_ref, w1_ref, b1_ref, w2_ref, b2_ref, out_ref, w1b, w2s, w2b):
    b = pl.program_id(0)
    n_real = nreal_ref[0]
    rows = xs_ref.shape[0]
    dff = w2_ref.shape[0]
    half = LANES // 2
    prev_e = blke_ref[jnp.maximum(b - 1, 0)]

    @pl.when(jnp.logical_and(b < n_real, jnp.logical_or(b == 0, blke_ref[b] != prev_e)))
    def _():
        w1b[...] = w1_ref[...].astype(BF16)
        for cb in range(w2s.shape[0]):
            cols = slice(cb * LANES, (cb + 1) * LANES)
            for g in range(dff // LANES):
                w2s[cb, pl.ds(g * LANES, half, stride=2), :] = w2_ref[g * LANES:g * LANES + half, cols]
                w2s[cb, pl.ds(g * LANES + 1, half, stride=2), :] = w2_ref[g * LANES + half:(g + 1) * LANES, cols]
            w2b[:, cols] = w2s[cb].astype(BF16)

    @pl.when(b < n_real)
    def _():
        x = xs_ref[...].astype(BF16)
        u = _dot(x, w1b[...]) + b1_ref[...]
        even = (lax.broadcasted_iota(jnp.int32, (rows, LANES), 1) % 2) == 0
        acts = []
        for t in range(dff // LANES):
            ua = u[:, 2 * t * LANES:(2 * t + 1) * LANES]
            ub = u[:, (2 * t + 1) * LANES:(2 * t + 2) * LANES]
            glu = jnp.where(even, ua, pltpu.roll(ub, 1, 1))
            lin = jnp.where(even, pltpu.roll(ua, LANES - 1, 1), ub)
            glu = jnp.minimum(glu, SWIGLU_LIMIT)
            lin = jnp.clip(lin, -SWIGLU_LIMIT, SWIGLU_LIMIT)
            acts.append((glu * _sigmoid(SWIGLU_ALPHA * glu) * (lin + 1.0)).astype(BF16))
        act = jnp.concatenate(acts, axis=1)
        out_ref[...] = _dot(act, w2b[...]) + b2_ref[...]

    @pl.when(b >= n_real)
    def _():
        out_ref[...] = jnp.zeros(out_ref.shape, out_ref.dtype)


def _moe(blk_e, n_real, xs, w1, b1, w2, b2):
    n_blocks = blk_e.shape[0]
    d = xs.shape[1]
    dff = w2.shape[1]
    wspec = lambda a: pl.BlockSpec((None,) + a.shape[1:], lambda b, be, nr: (be[b], 0, 0))
    return pl.pallas_call(
        _moe_kernel,
        out_shape=jax.ShapeDtypeStruct((n_blocks * MOE_ROWS, d), F32),
        grid_spec=pltpu.PrefetchScalarGridSpec(
            num_scalar_prefetch=2,
            grid=(n_blocks,),
            in_specs=[pl.BlockSpec((MOE_ROWS, d), lambda b, be, nr: (jnp.minimum(b, nr[0] - 1), 0)),
                      wspec(w1), wspec(b1), wspec(w2), wspec(b2)],
            out_specs=pl.BlockSpec((MOE_ROWS, d), lambda b, be, nr: (b, 0)),
            scratch_shapes=[pltpu.VMEM((d, 2 * dff), BF16), pltpu.VMEM((d // LANES, dff, LANES), F32),
                            pltpu.VMEM((dff, d), BF16)],
        ),
        compiler_params=_cparams(("arbitrary",)),
        name="moe_experts",
    )(blk_e, n_real, xs, w1, b1, w2, b2)


def _combine_kernel(pos_ref, ys_hbm, x1_ref, tg_ref, p_ref, wple_ref, wpg_ref, gple_ref, gfin_ref,
                    yh_ref, yt_ref, buf, sem, *, nbh):
    i = pl.program_id(0)
    tm = x1_ref.shape[0]

    def row_copy(blk, slot, r, kk):
        src = pos_ref[(blk * tm + r) * TOP_K + kk]
        return pltpu.make_async_copy(ys_hbm.at[pl.ds(src, 1), :], buf.at[slot, kk, pl.ds(r, 1), :],
                                     sem.at[slot])

    @pl.when(i == 0)
    def _():
        def body(r, carry):
            for kk in range(TOP_K):
                row_copy(0, 0, r, kk).start()
            return carry
        lax.fori_loop(0, tm, body, 0)

    def run(prefetch):
        slot = i % 2
        for kk in range(TOP_K):
            pltpu.make_async_copy(ys_hbm.at[pl.ds(0, tm), :], buf.at[slot, kk], sem.at[slot]).wait()
        if prefetch:
            for r in range(tm):
                for kk in range(TOP_K):
                    row_copy(i + 1, 1 - slot, r, kk).start()
        tg = tg_ref[...]
        x2 = x1_ref[...]
        for kk in range(TOP_K):
            x2 = x2 + tg[:, kk:kk + 1] * buf[slot, kk]
        gate = _sigmoid(_dot(_rms(x2, gple_ref[...]).astype(BF16), wpg_ref[...]))
        x3 = x2 + _dot(p_ref[...].astype(BF16), wple_ref[...]) * gate
        y = _rms(x3, gfin_ref[...])

        @pl.when(i < nbh)
        def _():
            yh_ref[...] = y

        @pl.when(i >= nbh)
        def _():
            yt_ref[...] = y

    @pl.when(i + 1 < pl.num_programs(0))
    def _():
        run(True)

    @pl.when(i + 1 == pl.num_programs(0))
    def _():
        run(False)


def _combine(pos, ys, x1, tg, p_all, w_ple, w_pg, g_ple, g_fin, tm, n_head):
    n, d = x1.shape
    nbh = n_head // tm
    row = lambda i, ps: (i, 0)
    full = lambda a: pl.BlockSpec(a.shape, lambda i, ps: (0, 0))
    return pl.pallas_call(
        functools.partial(_combine_kernel, nbh=nbh),
        out_shape=(jax.ShapeDtypeStruct((n_head, d), F32), jax.ShapeDtypeStruct((n - n_head, d), F32)),
        grid_spec=pltpu.PrefetchScalarGridSpec(
            num_scalar_prefetch=1,
            grid=(n // tm,),
            in_specs=[pl.BlockSpec(memory_space=pl.ANY),
                      pl.BlockSpec((tm, d), row), pl.BlockSpec((tm, LANES), row),
                      pl.BlockSpec((tm, p_all.shape[1]), row),
                      full(w_ple), full(w_pg), full(g_ple), full(g_fin)],
            out_specs=(pl.BlockSpec((tm, d), lambda i, ps: (jnp.minimum(i, nbh - 1), 0)),
                       pl.BlockSpec((tm, d), lambda i, ps: (jnp.maximum(i - nbh, 0), 0))),
            scratch_shapes=[pltpu.VMEM((2, TOP_K, tm, d), F32), pltpu.SemaphoreType.DMA((2,))],
        ),
        compiler_params=_cparams(("arbitrary",)),
        name="combine_ple",
    )(pos, ys, x1, tg, p_all, w_ple, w_pg, g_ple, g_fin)


def _routing(top_i, n_tok):
    a = n_tok * TOP_K
    flat_e = top_i.reshape(a)
    onehot = (flat_e[:, None] == jnp.arange(N_EXPERTS, dtype=jnp.int32)[None, :]).astype(jnp.int32)
    rank = jnp.take_along_axis(jnp.cumsum(onehot, axis=0), flat_e[:, None], axis=1)[:, 0] - 1
    counts = jnp.sum(onehot, axis=0)
    padded = (counts + MOE_ROWS - 1) // MOE_ROWS * MOE_ROWS
    pad_end = jnp.cumsum(padded)
    pad_start = pad_end - padded
    dest = (pad_start[flat_e] + rank).astype(jnp.int32)
    cnt_end = (pad_start + counts).astype(jnp.int32)
    n_blocks = -(-a // MOE_ROWS) + N_EXPERTS
    starts = jnp.arange(n_blocks, dtype=jnp.int32) * MOE_ROWS
    blk_e = jnp.minimum(jnp.sum((pad_end[None, :] <= starts[:, None]).astype(jnp.int32), axis=1),
                        N_EXPERTS - 1).astype(jnp.int32)
    n_real = (pad_end[-1] // MOE_ROWS).astype(jnp.int32).reshape(1)
    return dest, cnt_end, pad_end.astype(jnp.int32), blk_e, n_real, n_blocks


def _pick_tile(n, prefs):
    for t in prefs:
        if n % t == 0:
            return t
    raise ValueError(f"no tile in {prefs} divides {n}")


def kernel(x_prompt, x_sample, cache_ckv, cache_kpe, state_hgrn, page_table, p_prompt, p_sample, hg_lb, g_mix, w_in, g_qnorm, w_uq, w_uk, w_uv, g_kvnorm, g_onorm, w_pa, w_pb, w_out, g_ffn, w_router, b_router, w1, b1, w2, b2, g_ple, w_ple, w_pg, g_final):
    n_p, t_p, d = x_prompt.shape
    n_s, t_s, _ = x_sample.shape
    depth = w_in.shape[0]
    assert depth == 1 and n_p == 1
    n_pages = page_table.shape[1]
    page = cache_ckv.shape[2]
    past = n_pages * page
    tok_p = n_p * t_p
    tok_s = n_s * t_s
    n_tok = tok_p + tok_s
    assert tok_p % Q_TILE == 0 and tok_s % Q_TILE == 0 and Q_TILE % t_s == 0
    qw = HG_HEADS * HG_DK
    vw = HG_HEADS * HG_DV
    row2 = lambda v: v.reshape(1, -1).astype(F32)

    lb = jnp.cumsum(jax.nn.softmax(hg_lb.astype(F32), axis=0), axis=0)[0].reshape(1, qw)
    wi = w_in[0]
    hg_cols = 2 * qw + 2 * vw
    mla_cols = MLA_Q_RANK + MLA_KV_RANK + MLA_ROPE
    w_h = wi[:, :hg_cols].astype(BF16)
    w_mla = wi[:, hg_cols:hg_cols + mla_cols].astype(BF16)
    w_g = wi[:, hg_cols + mla_cols:].astype(BF16)
    wq = w_uq[0].reshape(MLA_Q_RANK, MLA_HEADS, MLA_NOPE + MLA_ROPE)
    w_uqt = jnp.concatenate([wq[:, :, :MLA_NOPE].reshape(MLA_Q_RANK, -1),
                             wq[:, :, MLA_NOPE:].reshape(MLA_Q_RANK, -1)], axis=1).T.astype(BF16)
    w_ukh = jnp.transpose(w_uk[0], (1, 0, 2)).astype(BF16)
    w_uvh = jnp.transpose(w_uv[0], (1, 0, 2)).astype(BF16)
    w_uvt = jnp.transpose(w_uv[0], (1, 2, 0)).astype(BF16)
    w_r = jnp.zeros((d, LANES), F32).at[:, :N_EXPERTS].set(w_router[0].astype(F32))
    w_r3 = jnp.stack(_split3(w_r))
    b_r = jnp.full((1, LANES), -1e30, F32).at[0, :N_EXPERTS].set(b_router[0].astype(F32))

    half = MLA_ROPE // 2
    inv = ROPE_THETA ** (-np.arange(half, dtype=np.float64) / half)
    pos = np.concatenate([np.tile(np.arange(t_p), n_p), np.tile(past + np.arange(t_s), n_s)])
    ang = pos.astype(np.float64)[:, None] * inv[None, :]
    cos, sin = np.cos(ang).astype(np.float32), np.sin(ang).astype(np.float32)
    cos2 = np.concatenate([cos, cos], axis=1)
    sin2 = np.concatenate([-sin, sin], axis=1)

    x_ht = (x_prompt.reshape(tok_p, d), x_sample.reshape(tok_s, d))
    p_all = jnp.concatenate([p_prompt[0].reshape(tok_p, -1), p_sample[0].reshape(tok_s, -1)], axis=0)

    tm = _pick_tile(n_tok, (512, 256, 128))
    zh, gates, c_new, kpe_new, qlat_t, qpe_t, kcb, kpeb = _inproj(
        *x_ht, cos2, sin2, cos.T, sin.T, row2(g_mix[0]), w_h, w_mla, w_g, row2(g_qnorm[0]), w_uqt, w_ukh,
        row2(g_kvnorm[0]), tm)

    gon = row2(g_onorm[0])
    chunk_p = 64 if t_p % 64 == 0 else t_p
    step_p = _pick_tile(t_p, (4 * chunk_p, 2 * chunk_p, chunk_p))
    oa_p, st_p = _hgrn(zh[None], t_p, lb, gon,
                       jnp.zeros((n_p, HG_HEADS, HG_DK, HG_DV), F32), chunk_p, chunk_p, step_p)
    t_pad = -(-t_s // SUBLANES) * SUBLANES
    zh_s = jnp.pad(zh[tok_p:].reshape(n_s, t_s, -1), ((0, 0), (0, t_pad - t_s), (0, 0)))
    oa_s, st_s = _hgrn(zh_s, t_pad, lb, gon, state_hgrn[0].astype(F32), t_pad, t_s, t_pad)
    oa_ht = (oa_p.reshape(tok_p, vw), oa_s[:, :t_s].reshape(tok_s, vw))

    nq_p = tok_p // Q_TILE
    tk = _pick_tile(tok_p, (512, 256, 128))
    ob_p = _attn_prompt(qlat_t, qpe_t, kcb, kpeb, kcb[:tok_p].T, w_uvt, tk)

    def sample_rows(a):
        w = a.shape[1]
        a = a.reshape(-1, w, MLA_HEADS, Q_TILE // t_s, t_s)
        return jnp.transpose(a, (0, 3, 2, 4, 1)).reshape(n_s, MLA_HEADS * t_s, w).astype(F32)

    def new_keys(a):
        return jnp.pad(a.reshape(n_s, t_s, -1), ((0, 0), (0, NEW_PAD - t_s), (0, 0)))

    ob_s = _attn_sample(page_table, sample_rows(qlat_t[nq_p:]), sample_rows(qpe_t[nq_p:]),
                        cache_ckv[0], jnp.swapaxes(cache_kpe[0], 1, 2),
                        new_keys(c_new[tok_p:]), new_keys(kpe_new[tok_p:]), w_uvh, t_s)
    ob_ht = (ob_p, ob_s.reshape(tok_s, -1))

    x1, h2, top_i, top_g = _merge(oa_ht, ob_ht, x_ht, gates, w_pa[0].astype(BF16), w_pb[0].astype(BF16),
                                  w_out[0].astype(BF16), row2(g_ffn[0]), w_r3, b_r, tm)

    dest, cnt_end, pad_end, blk_e, n_real, n_blocks = _routing(top_i[:, :TOP_K], n_tok)
    tc = _pick_tile(n_tok, (256, 128))
    xs = _dispatch(dest, cnt_end, pad_end, n_real, h2, n_blocks * MOE_ROWS, tm)
    ys = _moe(blk_e, n_real, xs, w1[0].astype(F32), b1[0][:, None, :].astype(F32),
              w2[0].astype(F32), b2[0][:, None, :].astype(F32))

    y_p, y_s = _combine(dest, ys, x1, top_g, p_all, w_ple[0].astype(BF16), w_pg[0].astype(BF16),
                        row2(g_ple[0]), row2(g_final), tc, tok_p)

    y_prompt = y_p.reshape(n_p, t_p, d)
    y_sample = y_s.reshape(n_s, t_s, d)
    return (y_prompt, y_sample,
            c_new[:tok_p].reshape(1, n_p, t_p, -1), kpe_new[:tok_p].reshape(1, n_p, t_p, -1), st_p[None],
            c_new[tok_p:].reshape(1, n_s, t_s, -1), kpe_new[tok_p:].reshape(1, n_s, t_s, -1), st_s[None])
```

```python
import functools
import math

import numpy as np
import jax
import jax.numpy as jnp
from jax import lax
from jax.experimental import pallas as pl
from jax.experimental.pallas import tpu as pltpu

F32 = jnp.float32
BF16 = jnp.bfloat16

HG_HEADS = 4
HG_DK = 128
HG_DV = 128
MLA_HEADS = 4
MLA_NOPE = 128
MLA_ROPE = 64
MLA_V = 128
MLA_Q_RANK = 384
MLA_KV_RANK = 256
MLA_SCALE = (MLA_NOPE + MLA_ROPE) ** -0.5
Q_SCALE = MLA_SCALE * math.log2(math.e)
ROPE_THETA = 10000.0
N_EXPERTS = 32
TOP_K = 4
SWIGLU_ALPHA = 1.702
SWIGLU_LIMIT = 7.0
EPS = 1e-6

LANES = 128
SUBLANES = 8
VMEM_LIMIT = 56 * 1024 * 1024

Q_TILE = 512
MOE_ROWS = 512
NEG_INF = float("-inf")


def _cparams(sem):
    return pltpu.CompilerParams(dimension_semantics=sem, vmem_limit_bytes=VMEM_LIMIT)


def _dot(a, b):
    return jnp.dot(a, b, preferred_element_type=F32)


def _dot_nt(a, b):
    return lax.dot_general(a, b, (((1,), (1,)), ((), ())), preferred_element_type=F32)


def _dot_tn(a, b):
    return lax.dot_general(a, b, (((0,), (0,)), ((), ())), preferred_element_type=F32)


def _rms(x, g):
    return x * lax.rsqrt(jnp.mean(x * x, axis=-1, keepdims=True) + EPS) * g


def _sigmoid(x):
    return 1.0 / (1.0 + jnp.exp(-x))


def _split3(x):
    hi = x.astype(BF16)
    r1 = x - hi.astype(F32)
    mid = r1.astype(BF16)
    lo = (r1 - mid.astype(F32)).astype(BF16)
    return hi, mid, lo


def _rope64(v, cos2, sin2):
    half = MLA_ROPE // 2
    partner = jnp.concatenate([v[:, half:], v[:, :half]], axis=1)
    return v * cos2 + partner * sin2


def _head_tail_specs(head, tail, tm):
    nbh = head.shape[0] // tm
    assert head.shape[0] % tm == 0 and tail.shape[0] % tm == 0
    return [pl.BlockSpec((tm, head.shape[1]), lambda i, *_: (jnp.minimum(i, nbh - 1), 0)),
            pl.BlockSpec((tm, tail.shape[1]), lambda i, *_: (jnp.maximum(i - nbh, 0), 0))]


def _head_tail_block(head_ref, tail_ref, nbh):
    return jnp.where(pl.program_id(0) < nbh, head_ref[...], tail_ref[...])


def _inproj_kernel(xh_ref, xt_ref, cos_ref, sin_ref, cost_ref, sint_ref, gmix_ref, wh_ref, wmla_ref, wg_ref,
                   gq_ref, wuqt_ref, wuk_ref, gkv_ref,
                   zh_ref, gates_ref, c_ref, kpe_ref, qlat_ref, qpe_ref, kcb_ref, kpeb_ref, *, nbh):
    tm = xh_ref.shape[0]
    half = MLA_ROPE // 2
    h = _rms(_head_tail_block(xh_ref, xt_ref, nbh), gmix_ref[...]).astype(BF16)
    zh_ref[...] = _dot(h, wh_ref[...])
    gates_ref[...] = _sigmoid(_dot(h, wg_ref[...]))
    zm = _dot(h, wmla_ref[...])
    c = _rms(zm[:, MLA_Q_RANK:MLA_Q_RANK + MLA_KV_RANK], gkv_ref[...])
    c_ref[...] = c
    kcb_ref[...] = c.astype(BF16)
    kpe = _rope64(zm[:, MLA_Q_RANK + MLA_KV_RANK:], cos_ref[...], sin_ref[...])
    kpe_ref[...] = kpe
    kpeb_ref[...] = kpe.astype(BF16)
    qn_t = _rms(zm[:, :MLA_Q_RANK], gq_ref[...]).T.astype(BF16)
    q_t = _dot(wuqt_ref[...], qn_t)
    cos_t = cost_ref[...]
    sin_t = sint_ref[...]
    for hh in range(MLA_HEADS):
        q_nope = q_t[hh * MLA_NOPE:(hh + 1) * MLA_NOPE].astype(BF16)
        qlat = (_dot(wuk_ref[hh], q_nope) * Q_SCALE).astype(BF16)
        off = MLA_HEADS * MLA_NOPE + hh * MLA_ROPE
        x1 = q_t[off:off + half]
        x2 = q_t[off + half:off + MLA_ROPE]
        qpe = (jnp.concatenate([x1 * cos_t - x2 * sin_t, x1 * sin_t + x2 * cos_t], axis=0)
               * Q_SCALE).astype(BF16)
        for tb in range(tm // Q_TILE):
            qlat_ref[tb, :, hh * Q_TILE:(hh + 1) * Q_TILE] = qlat[:, tb * Q_TILE:(tb + 1) * Q_TILE]
            qpe_ref[tb, :, hh * Q_TILE:(hh + 1) * Q_TILE] = qpe[:, tb * Q_TILE:(tb + 1) * Q_TILE]


def _inproj(x_head, x_tail, cos2, sin2, cos_t, sin_t, g_mix, w_h, w_mla, w_g, g_q, w_uqt, w_ukh, g_kv, tm):
    n, d = x_head.shape[0] + x_tail.shape[0], x_head.shape[1]
    nq = n // Q_TILE
    half = MLA_ROPE // 2
    rows = MLA_HEADS * Q_TILE
    row = lambda i: (i, 0)
    col = lambda i: (0, i)
    const2 = lambda i: (0, 0)
    const3 = lambda i: (0, 0, 0)
    full = lambda a: pl.BlockSpec(a.shape, const2 if a.ndim == 2 else const3)
    out_shape = (
        jax.ShapeDtypeStruct((n, w_h.shape[1]), F32),
        jax.ShapeDtypeStruct((n, w_g.shape[1]), F32),
        jax.ShapeDtypeStruct((n, MLA_KV_RANK), F32),
        jax.ShapeDtypeStruct((n, MLA_ROPE), F32),
        jax.ShapeDtypeStruct((nq, MLA_KV_RANK, rows), BF16),
        jax.ShapeDtypeStruct((nq, MLA_ROPE, rows), BF16),
        jax.ShapeDtypeStruct((n, MLA_KV_RANK), BF16),
        jax.ShapeDtypeStruct((n, MLA_ROPE), BF16),
    )
    tq = tm // Q_TILE
    out_specs = (
        pl.BlockSpec((tm, w_h.shape[1]), row),
        pl.BlockSpec((tm, w_g.shape[1]), row),
        pl.BlockSpec((tm, MLA_KV_RANK), row),
        pl.BlockSpec((tm, MLA_ROPE), row),
        pl.BlockSpec((tq, MLA_KV_RANK, rows), lambda i: (i, 0, 0)),
        pl.BlockSpec((tq, MLA_ROPE, rows), lambda i: (i, 0, 0)),
        pl.BlockSpec((tm, MLA_KV_RANK), row),
        pl.BlockSpec((tm, MLA_ROPE), row),
    )
    in_specs = _head_tail_specs(x_head, x_tail, tm) + [
        pl.BlockSpec((tm, MLA_ROPE), row),
        pl.BlockSpec((tm, MLA_ROPE), row),
        pl.BlockSpec((half, tm), col),
        pl.BlockSpec((half, tm), col),
        full(g_mix), full(w_h), full(w_mla), full(w_g), full(g_q), full(w_uqt), full(w_ukh), full(g_kv),
    ]
    return pl.pallas_call(
        functools.partial(_inproj_kernel, nbh=x_head.shape[0] // tm),
        out_shape=out_shape,
        grid=(n // tm,),
        in_specs=in_specs,
        out_specs=out_specs,
        compiler_params=_cparams(("parallel",)),
        name="inproj",
    )(x_head, x_tail, cos2, sin2, cos_t, sin_t, g_mix, w_h, w_mla, w_g, g_q, w_uqt, w_ukh, g_kv)


HG_DIAG = 8


def _hgrn_levels(chunk):
    levels, m = [], chunk // 2
    while m >= HG_DIAG:
        levels.append(m)
        m //= 2
    return tuple(levels)


def _hgrn_mats(chunk):
    r = np.arange(chunk)[:, None]
    j = np.arange(chunk)[None, :]
    mats = [j <= r, j > r]
    for m in _hgrn_levels(chunk):
        same = (r // (2 * m)) == (j // (2 * m))
        second = (r % (2 * m)) >= m
        mid = (r // (2 * m)) * 2 * m + m
        a = second & same & (j >= mid) & (j <= r)
        b = (~second) & same & (j > r) & (j < mid)
        mats.append(a | b)
    return np.concatenate(mats, axis=0).astype(np.float32)


def _hgrn_kernel(zh_ref, mats_ref, lb_ref, gon_ref, s0_ref, o_ref, sout_ref, st_ref, *, chunk, t_valid):
    ci = pl.program_id(1)
    n_seq = zh_ref.shape[0]

    @pl.when(ci == 0)
    def _():
        for s in range(n_seq):
            for h in range(HG_HEADS):
                st_ref[s * HG_HEADS + h] = s0_ref[s, h].T

    for s in range(n_seq):
        for r0 in range(0, zh_ref.shape[1], chunk):
            _hgrn_chunk(zh_ref, mats_ref, lb_ref, gon_ref, o_ref, st_ref, s, r0, chunk, t_valid)

    @pl.when(ci == pl.num_programs(1) - 1)
    def _():
        for s in range(n_seq):
            for h in range(HG_HEADS):
                sout_ref[s, h] = st_ref[s * HG_HEADS + h].T


def _hgrn_chunk(zh_ref, mats_ref, lb_ref, gon_ref, o_ref, st_ref, s, r0, chunk, t_valid):
    levels = _hgrn_levels(chunk)
    qw = HG_HEADS * HG_DK
    zh = zh_ref[s, r0:r0 + chunk]
    zq, zf, zi, zg = zh[:, :qw], zh[:, qw:2 * qw], zh[:, 2 * qw:3 * qw], zh[:, 3 * qw:]
    lb = lb_ref[...]
    f = lb + (1.0 - lb) * _sigmoid(zf)
    k = 1.0 - f
    logf = jnp.log(f)
    row = lax.broadcasted_iota(jnp.int32, (chunk, 1), 0)
    if t_valid < chunk:
        live = row < t_valid
        logf = jnp.where(live, logf, 0.0)
        k = jnp.where(live, k, 0.0)
    hi, mid, lo = _split3(logf)
    mats = mats_ref[...]
    e_all = _dot(mats, hi) + _dot(mats, mid) + _dot(mats, lo)
    b = e_all[:chunk]
    e_end = e_all[chunk:2 * chunk]
    b_end = b[chunk - 1:chunk, :]
    q = zq * (HG_DK ** -0.5)
    rr = lax.broadcasted_iota(jnp.int32, (chunk, chunk), 0)
    cc = lax.broadcasted_iota(jnp.int32, (chunk, chunk), 1)
    gon = gon_ref[...]

    for h in range(HG_HEADS):
        hs = slice(h * HG_DK, (h + 1) * HG_DK)
        qh, kh, vh, bh = q[:, hs], k[:, hs], zi[:, hs], b[:, hs]
        st = st_ref[s * HG_HEADS + h]
        o = _dot_nt((qh * jnp.exp(bh)).astype(BF16), st.astype(BF16))
        if levels:
            att = jnp.zeros((chunk, chunk), F32)
            for li, m in enumerate(levels):
                w = jnp.exp(e_all[(2 + li) * chunk:(3 + li) * chunk, hs])
                second = (row % (2 * m)) >= m
                qm = jnp.where(second, qh * w, 0.0).astype(BF16)
                km = jnp.where(second, 0.0, kh * w).astype(BF16)
                a = _dot_nt(qm, km)
                att = att + jnp.where((rr // (2 * m)) == (cc // (2 * m)), a, 0.0)
            o = o + _dot(att.astype(BF16), vh.astype(BF16))
        for dlt in range(min(HG_DIAG, chunk)):
            if dlt == 0:
                a = jnp.sum(qh * kh, axis=-1, keepdims=True)
                o = o + a * vh
            else:
                bs = pltpu.roll(bh, dlt, 0)
                ks = pltpu.roll(kh, dlt, 0)
                vs = pltpu.roll(vh, dlt, 0)
                w = jnp.where((row % HG_DIAG) >= dlt, jnp.exp(bh - bs), 0.0)
                a = jnp.sum(qh * ks * w, axis=-1, keepdims=True)
                o = o + a * vs
        on = _rms(o, gon)
        zgh = zg[:, hs]
        o_ref[s, r0:r0 + chunk, hs] = (on * (zgh * _sigmoid(zgh))).astype(o_ref.dtype)
        ke = (kh * jnp.exp(e_end[:, hs])).astype(BF16)
        st_ref[s * HG_HEADS + h] = st * jnp.exp(b_end[:, hs]) + _dot_tn(vh.astype(BF16), ke)


def _hgrn(zh, t, lb, g_onorm, s0, chunk, t_valid, step, n_seq):
    n, _, w = zh.shape
    assert n % n_seq == 0
    mats = jnp.asarray(_hgrn_mats(chunk), BF16)
    kern = functools.partial(_hgrn_kernel, chunk=chunk, t_valid=t_valid)
    vw = HG_HEADS * HG_DV
    return pl.pallas_call(
        kern,
        out_shape=(jax.ShapeDtypeStruct((n, t, vw), BF16),
                   jax.ShapeDtypeStruct(s0.shape, F32)),
        grid=(n // n_seq, t // step),
        in_specs=[
            pl.BlockSpec((n_seq, step, w), lambda i, c: (i, c, 0)),
            pl.BlockSpec(mats.shape, lambda i, c: (0, 0)),
            pl.BlockSpec(lb.shape, lambda i, c: (0, 0)),
            pl.BlockSpec(g_onorm.shape, lambda i, c: (0, 0)),
            pl.BlockSpec((n_seq,) + s0.shape[1:], lambda i, c: (i, 0, 0, 0)),
        ],
        out_specs=(
            pl.BlockSpec((n_seq, step, vw), lambda i, c: (i, c, 0)),
            pl.BlockSpec((n_seq,) + s0.shape[1:], lambda i, c: (i, 0, 0, 0)),
        ),
        scratch_shapes=[pltpu.VMEM((n_seq * HG_HEADS, HG_DV, HG_DK), F32)],
        compiler_params=_cparams(("arbitrary", "arbitrary")),
        name="hgrn2",
    )(zh, mats, lb, g_onorm, s0)


def _attn_prompt_kernel(qlat_ref, qpe_ref, kc_ref, kpe_ref, kct_ref, wuvt_ref, o_ref,
                        s0_ref, s1_ref, m_ref, l_ref, acc_ref, *, tk):
    i = pl.program_id(0)
    rows = MLA_HEADS * Q_TILE
    n_blk = (i * Q_TILE) // tk + 1
    acc_ref[...] = jnp.zeros(acc_ref.shape, F32)

    def scores(j, s_ref):
        off = pl.multiple_of(j * tk, tk)
        s_ref[...] = (_dot(kc_ref[pl.ds(off, tk), :], qlat_ref[0])
                      + _dot(kpe_ref[pl.ds(off, tk), :], qpe_ref[0]))

    def consume(j, s_ref, m_old, l_old, masked):
        off = pl.multiple_of(j * tk, tk)
        s = s_ref[...]
        if masked:
            k_pos = off + lax.broadcasted_iota(jnp.int32, s.shape, 0)
            q_pos = i * Q_TILE + lax.broadcasted_iota(jnp.int32, s.shape, 1) % Q_TILE
            s = jnp.where(k_pos <= q_pos, s, NEG_INF)
        m_new = jnp.maximum(m_old, jnp.max(s, axis=0, keepdims=True))
        p = jnp.exp2(s - m_new)
        alpha = jnp.exp2(m_old - m_new)
        l_new = alpha * l_old + jnp.sum(p, axis=0, keepdims=True)
        acc_ref[...] = alpha * acc_ref[...] + _dot(kct_ref[:, pl.ds(off, tk)], p.astype(BF16))
        return m_new, l_new

    scores(0, s0_ref)
    n_pairs = (n_blk - 1) // 2

    def body(t, carry):
        m, l = carry
        scores(2 * t + 1, s1_ref)
        m, l = consume(2 * t, s0_ref, m, l, False)
        scores(2 * t + 2, s0_ref)
        m, l = consume(2 * t + 1, s1_ref, m, l, False)
        return m, l

    m, l = lax.fori_loop(0, n_pairs, body,
                         (jnp.full((1, rows), NEG_INF, F32), jnp.zeros((1, rows), F32)))
    m_ref[...] = m
    l_ref[...] = l
    last = n_blk - 1

    @pl.when(2 * n_pairs == last)
    def _():
        _, l2 = consume(last, s0_ref, m_ref[...], l_ref[...], True)
        l_ref[...] = l2

    @pl.when(2 * n_pairs != last)
    def _():
        scores(last, s1_ref)
        m1, l1 = consume(last - 1, s0_ref, m_ref[...], l_ref[...], False)
        _, l2 = consume(last, s1_ref, m1, l1, True)
        l_ref[...] = l2

    o_lat = (acc_ref[...] / l_ref[...]).astype(BF16)
    for h in range(MLA_HEADS):
        ob_t = _dot(wuvt_ref[h], o_lat[:, h * Q_TILE:(h + 1) * Q_TILE])
        o_ref[:, h * MLA_V:(h + 1) * MLA_V] = ob_t.T.astype(o_ref.dtype)


def _attn_prompt(qlat_t, qpe_t, kc, kpe, kct, w_uvt, tk):
    t = kct.shape[1]
    nq = t // Q_TILE
    rows = MLA_HEADS * Q_TILE
    kern = functools.partial(_attn_prompt_kernel, tk=tk)
    resident = lambda shape: pl.BlockSpec(shape, lambda i: (0,) * len(shape), pipeline_mode=pl.Buffered(1))
    return pl.pallas_call(
        kern,
        out_shape=jax.ShapeDtypeStruct((nq * Q_TILE, MLA_HEADS * MLA_V), BF16),
        grid=(nq,),
        in_specs=[
            pl.BlockSpec((1, MLA_KV_RANK, rows), lambda i: (i, 0, 0)),
            pl.BlockSpec((1, MLA_ROPE, rows), lambda i: (i, 0, 0)),
            resident((t, MLA_KV_RANK)),
            resident((t, MLA_ROPE)),
            resident((MLA_KV_RANK, t)),
            resident(w_uvt.shape),
        ],
        out_specs=pl.BlockSpec((Q_TILE, MLA_HEADS * MLA_V), lambda i: (i, 0)),
        scratch_shapes=[pltpu.VMEM((tk, rows), F32), pltpu.VMEM((tk, rows), F32),
                        pltpu.VMEM((1, rows), F32), pltpu.VMEM((1, rows), F32),
                        pltpu.VMEM((MLA_KV_RANK, rows), F32)],
        compiler_params=_cparams(("parallel",)),
        name="attn_prompt",
    )(qlat_t, qpe_t, kc, kpe, kct, w_uvt)


NEW_PAD = 16


def _attn_sample_kernel(pt_ref, q1_ref, q2_ref, ckv_hbm, kpet_hbm, cn_ref, kn_ref, wuv_ref, o_ref,
                        cbuf, pbuf, spe, sem, *, t_new):
    n = pl.program_id(0)
    n_pages = cbuf.shape[1]
    page = cbuf.shape[2]

    def issue(seq, slot):
        def body(k, carry):
            pg = pt_ref[seq, k]
            pltpu.make_async_copy(ckv_hbm.at[pg], cbuf.at[slot, k], sem.at[0, slot]).start()
            pltpu.make_async_copy(kpet_hbm.at[pg], pbuf.at[slot, k], sem.at[1, slot]).start()
            return carry
        lax.fori_loop(0, n_pages, body, 0, unroll=4)

    @pl.when(n == 0)
    def _():
        issue(0, 0)

    @pl.when(n + 1 < pl.num_programs(0))
    def _():
        issue(n + 1, (n + 1) % 2)

    slot = n % 2
    pltpu.make_async_copy(ckv_hbm.at[pl.ds(0, n_pages)], cbuf.at[slot], sem.at[0, slot]).wait()
    pltpu.make_async_copy(kpet_hbm.at[pl.ds(0, n_pages)], pbuf.at[slot], sem.at[1, slot]).wait()

    q1 = q1_ref[0]
    q2 = q2_ref[0]
    c = cbuf[slot].reshape(n_pages * page, MLA_KV_RANK)
    for k in range(n_pages):
        spe[:, k * page:(k + 1) * page] = _dot(q2, pbuf[slot, k])
    s = _dot_nt(q1, c) + spe[...]
    cn = cn_ref[0]
    s2 = _dot_nt(q1, cn) + _dot_nt(q2, kn_ref[0])
    t_q = lax.broadcasted_iota(jnp.int32, s2.shape, 0) % t_new
    t_k = lax.broadcasted_iota(jnp.int32, s2.shape, 1)
    s2 = jnp.where(t_k <= t_q, s2, NEG_INF)
    m = jnp.maximum(jnp.max(s, axis=-1, keepdims=True), jnp.max(s2, axis=-1, keepdims=True))
    p = jnp.exp2(s - m)
    p2 = jnp.exp2(s2 - m)
    l = jnp.sum(p, axis=-1, keepdims=True) + jnp.sum(p2, axis=-1, keepdims=True)
    o_lat = ((_dot(p, c) + _dot(p2, cn)) / l).astype(BF16)
    for h in range(MLA_HEADS):
        o_ref[0, :, h * MLA_V:(h + 1) * MLA_V] = _dot(
            o_lat[h * t_new:(h + 1) * t_new], wuv_ref[h]).astype(o_ref.dtype)


def _attn_sample(page_table, q1, q2, ckv, kpet, cn, kn, w_uv, t_new):
    n, rows, _ = q1.shape
    n_pages = page_table.shape[1]
    page = ckv.shape[1]
    kern = functools.partial(_attn_sample_kernel, t_new=t_new)
    return pl.pallas_call(
        kern,
        out_shape=jax.ShapeDtypeStruct((n, t_new, MLA_HEADS * MLA_V), BF16),
        grid_spec=pltpu.PrefetchScalarGridSpec(
            num_scalar_prefetch=1,
            grid=(n,),
            in_specs=[
                pl.BlockSpec((1, rows, MLA_KV_RANK), lambda i, pt: (i, 0, 0)),
                pl.BlockSpec((1, rows, MLA_ROPE), lambda i, pt: (i, 0, 0)),
                pl.BlockSpec(memory_space=pl.ANY),
                pl.BlockSpec(memory_space=pl.ANY),
                pl.BlockSpec((1, NEW_PAD, MLA_KV_RANK), lambda i, pt: (i, 0, 0)),
                pl.BlockSpec((1, NEW_PAD, MLA_ROPE), lambda i, pt: (i, 0, 0)),
                pl.BlockSpec(w_uv.shape, lambda i, pt: (0, 0, 0)),
            ],
            out_specs=pl.BlockSpec((1, t_new, MLA_HEADS * MLA_V), lambda i, pt: (i, 0, 0)),
            scratch_shapes=[pltpu.VMEM((2, n_pages, page, MLA_KV_RANK), F32),
                            pltpu.VMEM((2, n_pages, MLA_ROPE, page), F32),
                            pltpu.VMEM((rows, n_pages * page), F32),
                            pltpu.SemaphoreType.DMA((2, 2))],
        ),
        compiler_params=_cparams(("arbitrary",)),
        name="attn_sample",
    )(page_table, q1, q2, ckv, kpet, cn, kn, w_uv)


def _merge_kernel(oah_ref, oat_ref, obh_ref, obt_ref, xh_ref, xt_ref, gates_ref, wpa_ref, wpb_ref, wout_ref,
                  gffn_ref, wr_ref, br_ref, x1_ref, h2_ref, ti_ref, tg_ref, *, nbh):
    d = xh_ref.shape[1]
    gates = gates_ref[...]
    o_a = _head_tail_block(oah_ref, oat_ref, nbh)
    o_b = _head_tail_block(obh_ref, obt_ref, nbh)
    mix = gates[:, :d] * _dot(o_a, wpa_ref[...]) + gates[:, d:] * _dot(o_b, wpb_ref[...])
    x1 = _head_tail_block(xh_ref, xt_ref, nbh) + _dot(mix.astype(BF16), wout_ref[...])
    x1_ref[...] = x1
    h2 = _rms(x1, gffn_ref[...])
    h2_ref[...] = h2
    h_hi, h_mid, h_lo = _split3(h2)
    w_hi, w_mid, w_lo = wr_ref[0], wr_ref[1], wr_ref[2]
    logits = (_dot(h_hi, w_hi) + (_dot(h_hi, w_mid) + _dot(h_mid, w_hi))
              + (_dot(h_mid, w_mid) + _dot(h_hi, w_lo) + _dot(h_lo, w_hi))) + br_ref[...]
    lane = lax.broadcasted_iota(jnp.int32, logits.shape, 1)
    vals, idxs = [], []
    cur = logits
    for _ in range(TOP_K):
        mx = jnp.max(cur, axis=-1, keepdims=True)
        ix = jnp.min(jnp.where(cur == mx, lane, LANES), axis=-1, keepdims=True)
        vals.append(mx)
        idxs.append(ix)
        cur = jnp.where(lane == ix, NEG_INF, cur)
    es = [jnp.exp(v - vals[0]) for v in vals]
    tot = es[0] + es[1] + es[2] + es[3]
    ti = jnp.zeros(logits.shape, jnp.int32)
    tg = jnp.zeros(logits.shape, F32)
    for kk in range(TOP_K):
        ti = jnp.where(lane == kk, idxs[kk], ti)
        tg = jnp.where(lane == kk, es[kk] / tot, tg)
    ti_ref[...] = ti
    tg_ref[...] = tg


def _merge(oa_ht, ob_ht, x_ht, gates, w_pa, w_pb, w_out, g_ffn, w_r3, b_r, tm):
    n, d = gates.shape[0], x_ht[0].shape[1]
    row = lambda i: (i, 0)
    full = lambda a: pl.BlockSpec(a.shape, (lambda i: (0, 0)) if a.ndim == 2 else (lambda i: (0, 0, 0)))
    return pl.pallas_call(
        functools.partial(_merge_kernel, nbh=x_ht[0].shape[0] // tm),
        out_shape=(jax.ShapeDtypeStruct((n, d), F32), jax.ShapeDtypeStruct((n, d), F32),
                   jax.ShapeDtypeStruct((n, LANES), jnp.int32), jax.ShapeDtypeStruct((n, LANES), F32)),
        grid=(n // tm,),
        in_specs=(_head_tail_specs(*oa_ht, tm) + _head_tail_specs(*ob_ht, tm) + _head_tail_specs(*x_ht, tm)
                  + [pl.BlockSpec((tm, gates.shape[1]), row),
                     full(w_pa), full(w_pb), full(w_out), full(g_ffn), full(w_r3), full(b_r)]),
        out_specs=(pl.BlockSpec((tm, d), row), pl.BlockSpec((tm, d), row),
                   pl.BlockSpec((tm, LANES), row), pl.BlockSpec((tm, LANES), row)),
        compiler_params=_cparams(("parallel",)),
        name="merge_router",
    )(*oa_ht, *ob_ht, *x_ht, gates, w_pa, w_pb, w_out, g_ffn, w_r3, b_r)


def _dispatch_kernel(dest_ref, cend_ref, pend_ref, nreal_ref, h2_ref, xs_ref, zblk, sem, zsem):
    i = pl.program_id(0)
    tm = h2_ref.shape[0]
    blk = zblk.shape[0]

    @pl.when(i == 0)
    def _():
        zblk[...] = jnp.zeros(zblk.shape, zblk.dtype)

        def zero_row(r):
            return pltpu.make_async_copy(zblk.at[pl.ds(0, 1), :], xs_ref.at[pl.ds(r, 1), :], zsem)

        def zero_rows(r, size):
            return pltpu.make_async_copy(zblk.at[pl.ds(0, size), :],
                                         xs_ref.at[pl.ds(pl.multiple_of(r, SUBLANES), size), :], zsem)

        def run(lo, hi, copy, go):
            def body(r, carry):
                go(copy(r))
                return carry
            lax.fori_loop(lo, hi, body, 0)

        sizes = [1 << k for k in range(blk.bit_length() - 1, SUBLANES.bit_length() - 2, -1)]
        for go in (lambda c: c.start(), lambda c: c.wait()):
            for e in range(N_EXPERTS):
                lo = cend_ref[e]
                aligned = jnp.minimum((lo + SUBLANES - 1) // SUBLANES * SUBLANES, pend_ref[e])
                run(lo, aligned, zero_row, go)
                left = pend_ref[e] - aligned
                off = aligned
                for size in sizes:
                    take = (left & size) != 0

                    @pl.when(take)
                    def _():
                        go(zero_rows(off, size))
                    off = off + jnp.where(take, size, 0)
            run(nreal_ref[0], xs_ref.shape[0] // blk, lambda j: zero_rows(j * blk, blk), go)

    def body(r, carry):
        for kk in range(TOP_K):
            dst = dest_ref[(i * tm + r) * TOP_K + kk]
            pltpu.make_async_copy(h2_ref.at[pl.ds(r, 1), :], xs_ref.at[pl.ds(dst, 1), :], sem).start()
        return carry
    lax.fori_loop(0, tm, body, 0, unroll=4)
    for kk in range(TOP_K):
        pltpu.make_async_copy(h2_ref, xs_ref.at[pl.ds(0, tm), :], sem).wait()


def _dispatch(dest, cnt_end, pad_end, n_real, h2, n_rows, tm):
    n, d = h2.shape
    return pl.pallas_call(
        _dispatch_kernel,
        out_shape=jax.ShapeDtypeStruct((n_rows, d), h2.dtype),
        grid_spec=pltpu.PrefetchScalarGridSpec(
            num_scalar_prefetch=4,
            grid=(n // tm,),
            in_specs=[pl.BlockSpec((tm, d), lambda i, ds, ce, pe, nr: (i, 0))],
            out_specs=pl.BlockSpec(memory_space=pl.ANY),
            scratch_shapes=[pltpu.VMEM((MOE_ROWS, d), h2.dtype), pltpu.SemaphoreType.DMA(()),
                            pltpu.SemaphoreType.DMA(())],
        ),
        compiler_params=_cparams(("arbitrary",)),
        name="moe_dispatch",
    )(dest, cnt_end, pad_end, n_real, h2)


def _moe_kernel(blke_ref, nreal_ref, xs_ref, w1_ref, b1_ref, w2_ref, b2_ref, out_ref, w1b, w2s, w2b):
    b = pl.program_id(0)
    n_real = nreal_ref[0]
    rows = xs_ref.shape[0]
    dff = w2_ref.shape[0]
    half = LANES // 2
    prev_e = blke_ref[jnp.maximum(b - 1, 0)]

    @pl.when(jnp.logical_and(b < n_real, jnp.logical_or(b == 0, blke_ref[b] != prev_e)))
    def _():
        w1b[...] = w1_ref[...].astype(BF16)
        for cb in range(w2s.shape[0]):
            cols = slice(cb * LANES, (cb + 1) * LANES)
            for g in range(dff // LANES):
                w2s[cb, pl.ds(g * LANES, half, stride=2), :] = w2_ref[g * LANES:g * LANES + half, cols]
                w2s[cb, pl.ds(g * LANES + 1, half, stride=2), :] = w2_ref[g * LANES + half:(g + 1) * LANES, cols]
            w2b[:, cols] = w2s[cb].astype(BF16)

    @pl.when(b < n_real)
    def _():
        x = xs_ref[...].astype(BF16)
        u = _dot(x, w1b[...]) + b1_ref[...]
        even = (lax.broadcasted_iota(jnp.int32, (rows, LANES), 1) % 2) == 0
        acts = []
        for t in range(dff // LANES):
            ua = u[:, 2 * t * LANES:(2 * t + 1) * LANES]
            ub = u[:, (2 * t + 1) * LANES:(2 * t + 2) * LANES]
            glu = jnp.where(even, ua, pltpu.roll(ub, 1, 1))
            lin = jnp.where(even, pltpu.roll(ua, LANES - 1, 1), ub)
            glu = jnp.minimum(glu, SWIGLU_LIMIT)
            lin = jnp.clip(lin, -SWIGLU_LIMIT, SWIGLU_LIMIT)
            acts.append((glu * _sigmoid(SWIGLU_ALPHA * glu) * (lin + 1.0)).astype(BF16))
        act = jnp.concatenate(acts, axis=1)
        out_ref[...] = _dot(act, w2b[...]) + b2_ref[...]

    @pl.when(b >= n_real)
    def _():
        out_ref[...] = jnp.zeros(out_ref.shape, out_ref.dtype)


def _moe(blk_e, n_real, xs, w1, b1, w2, b2):
    n_blocks = blk_e.shape[0]
    d = xs.shape[1]
    dff = w2.shape[1]
    wspec = lambda a: pl.BlockSpec((None,) + a.shape[1:], lambda b, be, nr: (be[b], 0, 0))
    return pl.pallas_call(
        _moe_kernel,
        out_shape=jax.ShapeDtypeStruct((n_blocks * MOE_ROWS, d), F32),
        grid_spec=pltpu.PrefetchScalarGridSpec(
            num_scalar_prefetch=2,
            grid=(n_blocks,),
            in_specs=[pl.BlockSpec((MOE_ROWS, d), lambda b, be, nr: (jnp.minimum(b, nr[0] - 1), 0)),
                      wspec(w1), wspec(b1), wspec(w2), wspec(b2)],
            out_specs=pl.BlockSpec((MOE_ROWS, d), lambda b, be, nr: (b, 0)),
            scratch_shapes=[pltpu.VMEM((d, 2 * dff), BF16), pltpu.VMEM((d // LANES, dff, LANES), F32),
                            pltpu.VMEM((dff, d), BF16)],
        ),
        compiler_params=_cparams(("arbitrary",)),
        name="moe_experts",
    )(blk_e, n_real, xs, w1, b1, w2, b2)


def _combine_kernel(pos_ref, ys_hbm, x1_ref, tg_ref, ph_ref, pt_ref, wple_ref, wpg_ref, gple_ref, gfin_ref,
                    yh_ref, yt_ref, buf, sem, *, nbh):
    i = pl.program_id(0)
    tm = x1_ref.shape[0]

    def row_copy(blk, slot, r, kk):
        src = pos_ref[(blk * tm + r) * TOP_K + kk]
        return pltpu.make_async_copy(ys_hbm.at[pl.ds(src, 1), :], buf.at[slot, kk, pl.ds(r, 1), :],
                                     sem.at[slot])

    @pl.when(i == 0)
    def _():
        def body(r, carry):
            for kk in range(TOP_K):
                row_copy(0, 0, r, kk).start()
            return carry
        lax.fori_loop(0, tm, body, 0)

    def run(prefetch):
        slot = i % 2
        for kk in range(TOP_K):
            pltpu.make_async_copy(ys_hbm.at[pl.ds(0, tm), :], buf.at[slot, kk], sem.at[slot]).wait()
        if prefetch:
            for r in range(tm):
                for kk in range(TOP_K):
                    row_copy(i + 1, 1 - slot, r, kk).start()
        tg = tg_ref[...]
        x2 = x1_ref[...]
        for kk in range(TOP_K):
            x2 = x2 + tg[:, kk:kk + 1] * buf[slot, kk]
        gate = _sigmoid(_dot(_rms(x2, gple_ref[...]).astype(BF16), wpg_ref[...]))
        x3 = x2 + _dot(_head_tail_block(ph_ref, pt_ref, nbh).astype(BF16), wple_ref[...]) * gate
        y = _rms(x3, gfin_ref[...])

        @pl.when(i < nbh)
        def _():
            yh_ref[...] = y

        @pl.when(i >= nbh)
        def _():
            yt_ref[...] = y

    @pl.when(i + 1 < pl.num_programs(0))
    def _():
        run(True)

    @pl.when(i + 1 == pl.num_programs(0))
    def _():
        run(False)


def _combine(pos, ys, x1, tg, p_ht, w_ple, w_pg, g_ple, g_fin, tm, n_head):
    n, d = x1.shape
    nbh = n_head // tm
    row = lambda i, ps: (i, 0)
    full = lambda a: pl.BlockSpec(a.shape, lambda i, ps: (0, 0))
    return pl.pallas_call(
        functools.partial(_combine_kernel, nbh=nbh),
        out_shape=(jax.ShapeDtypeStruct((n_head, d), F32), jax.ShapeDtypeStruct((n - n_head, d), F32)),
        grid_spec=pltpu.PrefetchScalarGridSpec(
            num_scalar_prefetch=1,
            grid=(n // tm,),
            in_specs=([pl.BlockSpec(memory_space=pl.ANY),
                       pl.BlockSpec((tm, d), row), pl.BlockSpec((tm, LANES), row)]
                      + _head_tail_specs(*p_ht, tm)
                      + [full(w_ple), full(w_pg), full(g_ple), full(g_fin)]),
            out_specs=(pl.BlockSpec((tm, d), lambda i, ps: (jnp.minimum(i, nbh - 1), 0)),
                       pl.BlockSpec((tm, d), lambda i, ps: (jnp.maximum(i - nbh, 0), 0))),
            scratch_shapes=[pltpu.VMEM((2, TOP_K, tm, d), F32), pltpu.SemaphoreType.DMA((2,))],
        ),
        compiler_params=_cparams(("arbitrary",)),
        name="combine_ple",
    )(pos, ys, x1, tg, *p_ht, w_ple, w_pg, g_ple, g_fin)


def _routing(top_i, n_tok):
    a = n_tok * TOP_K
    flat_e = top_i.reshape(a)
    onehot = (flat_e[:, None] == jnp.arange(N_EXPERTS, dtype=jnp.int32)[None, :]).astype(jnp.int32)
    rank = jnp.take_along_axis(jnp.cumsum(onehot, axis=0), flat_e[:, None], axis=1)[:, 0] - 1
    counts = jnp.sum(onehot, axis=0)
    padded = (counts + MOE_ROWS - 1) // MOE_ROWS * MOE_ROWS
    pad_end = jnp.cumsum(padded)
    pad_start = pad_end - padded
    dest = (pad_start[flat_e] + rank).astype(jnp.int32)
    cnt_end = (pad_start + counts).astype(jnp.int32)
    n_blocks = -(-a // MOE_ROWS) + N_EXPERTS
    starts = jnp.arange(n_blocks, dtype=jnp.int32) * MOE_ROWS
    blk_e = jnp.minimum(jnp.sum((pad_end[None, :] <= starts[:, None]).astype(jnp.int32), axis=1),
                        N_EXPERTS - 1).astype(jnp.int32)
    n_real = (pad_end[-1] // MOE_ROWS).astype(jnp.int32).reshape(1)
    return dest, cnt_end, pad_end.astype(jnp.int32), blk_e, n_real, n_blocks


def _pick_tile(n, prefs):
    for t in prefs:
        if n % t == 0:
            return t
    raise ValueError(f"no tile in {prefs} divides {n}")


def kernel(x_prompt, x_sample, cache_ckv, cache_kpe, state_hgrn, page_table, p_prompt, p_sample, hg_lb, g_mix, w_in, g_qnorm, w_uq, w_uk, w_uv, g_kvnorm, g_onorm, w_pa, w_pb, w_out, g_ffn, w_router, b_router, w1, b1, w2, b2, g_ple, w_ple, w_pg, g_final):
    n_p, t_p, d = x_prompt.shape
    n_s, t_s, _ = x_sample.shape
    depth = w_in.shape[0]
    assert depth == 1 and n_p == 1
    n_pages = page_table.shape[1]
    page = cache_ckv.shape[2]
    past = n_pages * page
    tok_p = n_p * t_p
    tok_s = n_s * t_s
    n_tok = tok_p + tok_s
    assert tok_p % Q_TILE == 0 and tok_s % Q_TILE == 0 and Q_TILE % t_s == 0
    qw = HG_HEADS * HG_DK
    vw = HG_HEADS * HG_DV
    row2 = lambda v: v.reshape(1, -1).astype(F32)

    lb = jnp.cumsum(jax.nn.softmax(hg_lb.astype(F32), axis=0), axis=0)[0].reshape(1, qw)
    wi = w_in[0]
    hg_cols = 2 * qw + 2 * vw
    mla_cols = MLA_Q_RANK + MLA_KV_RANK + MLA_ROPE
    w_h = wi[:, :hg_cols].astype(BF16)
    w_mla = wi[:, hg_cols:hg_cols + mla_cols].astype(BF16)
    w_g = wi[:, hg_cols + mla_cols:].astype(BF16)
    wq = w_uq[0].reshape(MLA_Q_RANK, MLA_HEADS, MLA_NOPE + MLA_ROPE)
    w_uqt = jnp.concatenate([wq[:, :, :MLA_NOPE].reshape(MLA_Q_RANK, -1),
                             wq[:, :, MLA_NOPE:].reshape(MLA_Q_RANK, -1)], axis=1).T.astype(BF16)
    w_ukh = jnp.transpose(w_uk[0], (1, 0, 2)).astype(BF16)
    w_uvh = jnp.transpose(w_uv[0], (1, 0, 2)).astype(BF16)
    w_uvt = jnp.transpose(w_uv[0], (1, 2, 0)).astype(BF16)
    w_r = jnp.zeros((d, LANES), F32).at[:, :N_EXPERTS].set(w_router[0].astype(F32))
    w_r3 = jnp.stack(_split3(w_r))
    b_r = jnp.full((1, LANES), -1e30, F32).at[0, :N_EXPERTS].set(b_router[0].astype(F32))

    half = MLA_ROPE // 2
    inv = ROPE_THETA ** (-np.arange(half, dtype=np.float64) / half)
    pos = np.concatenate([np.tile(np.arange(t_p), n_p), np.tile(past + np.arange(t_s), n_s)])
    ang = pos.astype(np.float64)[:, None] * inv[None, :]
    cos, sin = np.cos(ang).astype(np.float32), np.sin(ang).astype(np.float32)
    cos2 = np.concatenate([cos, cos], axis=1)
    sin2 = np.concatenate([-sin, sin], axis=1)

    x_ht = (x_prompt.reshape(tok_p, d), x_sample.reshape(tok_s, d))
    p_ht = (p_prompt[0].reshape(tok_p, -1), p_sample[0].reshape(tok_s, -1))

    tm = _pick_tile(n_tok, (512, 256, 128))
    zh, gates, c_new, kpe_new, qlat_t, qpe_t, kcb, kpeb = _inproj(
        *x_ht, cos2, sin2, cos.T, sin.T, row2(g_mix[0]), w_h, w_mla, w_g, row2(g_qnorm[0]), w_uqt, w_ukh,
        row2(g_kvnorm[0]), tm)

    gon = row2(g_onorm[0])
    chunk_p = 64 if t_p % 64 == 0 else t_p
    step_p = _pick_tile(t_p, (4 * chunk_p, 2 * chunk_p, chunk_p))
    oa_p, st_p = _hgrn(zh[None], t_p, lb, gon,
                       jnp.zeros((n_p, HG_HEADS, HG_DK, HG_DV), F32), chunk_p, chunk_p, step_p, 1)
    t_pad = -(-t_s // SUBLANES) * SUBLANES
    zh_s = jnp.pad(zh[tok_p:].reshape(n_s, t_s, -1), ((0, 0), (0, t_pad - t_s), (0, 0)))
    oa_s, st_s = _hgrn(zh_s, t_pad, lb, gon, state_hgrn[0].astype(F32), t_pad, t_s, t_pad,
                       _pick_tile(n_s, (4, 2, 1)))
    oa_ht = (oa_p.reshape(tok_p, vw), oa_s[:, :t_s].reshape(tok_s, vw))

    nq_p = tok_p // Q_TILE
    tk = _pick_tile(tok_p, (512, 256, 128))
    ob_p = _attn_prompt(qlat_t, qpe_t, kcb, kpeb, kcb[:tok_p].T, w_uvt, tk)

    def sample_rows(a):
        w = a.shape[1]
        a = a.reshape(-1, w, MLA_HEADS, Q_TILE // t_s, t_s)
        return jnp.transpose(a, (0, 3, 2, 4, 1)).reshape(n_s, MLA_HEADS * t_s, w).astype(F32)

    def new_keys(a):
        return jnp.pad(a.reshape(n_s, t_s, -1), ((0, 0), (0, NEW_PAD - t_s), (0, 0)))

    ob_s = _attn_sample(page_table, sample_rows(qlat_t[nq_p:]), sample_rows(qpe_t[nq_p:]),
                        cache_ckv[0], jnp.swapaxes(cache_kpe[0], 1, 2),
                        new_keys(c_new[tok_p:]), new_keys(kpe_new[tok_p:]), w_uvh, t_s)
    ob_ht = (ob_p, ob_s.reshape(tok_s, -1))

    x1, h2, top_i, top_g = _merge(oa_ht, ob_ht, x_ht, gates, w_pa[0].astype(BF16), w_pb[0].astype(BF16),
                                  w_out[0].astype(BF16), row2(g_ffn[0]), w_r3, b_r, tm)

    dest, cnt_end, pad_end, blk_e, n_real, n_blocks = _routing(top_i[:, :TOP_K], n_tok)
    tc = _pick_tile(n_tok, (256, 128))
    xs = _dispatch(dest, cnt_end, pad_end, n_real, h2, n_blocks * MOE_ROWS, tm)
    ys = _moe(blk_e, n_real, xs, w1[0].astype(F32), b1[0][:, None, :].astype(F32),
              w2[0].astype(F32), b2[0][:, None, :].astype(F32))

    y_p, y_s = _combine(dest, ys, x1, top_g, p_ht, w_ple[0].astype(BF16), w_pg[0].astype(BF16),
                        row2(g_ple[0]), row2(g_final), tc, tok_p)

    y_prompt = y_p.reshape(n_p, t_p, d)
    y_sample = y_s.reshape(n_s, t_s, d)
    return (y_prompt, y_sample,
            c_new[:tok_p].reshape(1, n_p, t_p, -1), kpe_new[:tok_p].reshape(1, n_p, t_p, -1), st_p[None],
            c_new[tok_p:].reshape(1, n_s, t_s, -1), kpe_new[tok_p:].reshape(1, n_s, t_s, -1), st_s[None])
```
